```python
import jax
import jax.numpy as jnp
from jax import lax
import numpy as np

D_MODEL = 2048
BATCH = 4
SEQ = 4096
DEPTH = 1
DEC_BATCH = 128
DEC_SEQ = 4
PAST_LEN = 16384
PAGE_SIZE = 128

MLA_HEADS = 8
MLA_NOPE = 64
MLA_ROPE = 32
MLA_V = 64
MLA_Q_RANK = 512
MLA_KV_RANK = 256
ROPE_THETA = 10000.0
FOX_HEADS = 8
FOX_KV_HEADS = 4
FOX_HEAD_DIM = 64
FORGET_BIAS = 3.0
N_GROUPS = 4
EXPERTS_PER_GROUP = 8
N_EXPERTS = N_GROUPS * EXPERTS_PER_GROUP
TOP_K = 2
D_EXPERT = 512
Q_BLOCK = 128
RMS_EPS = 1e-6
N_MOD = 6
IN_SIZES = (MLA_Q_RANK, MLA_KV_RANK, MLA_ROPE, FOX_HEADS * FOX_HEAD_DIM, FOX_KV_HEADS * FOX_HEAD_DIM,
            FOX_KV_HEADS * FOX_HEAD_DIM, FOX_HEADS, D_MODEL, D_MODEL)
IN_COLS = sum(IN_SIZES)
IN_OFFSETS = tuple(sum(IN_SIZES[:i + 1]) for i in range(len(IN_SIZES) - 1))

kernel_name = 'hybrid_mla_fox_hmoe_step'


def _rms(x, g):
    x32 = x.astype(jnp.float32)
    y = x32 * lax.rsqrt(jnp.mean(x32 * x32, axis=-1, keepdims=True) + RMS_EPS)
    return (y * g.astype(jnp.float32)).astype(x.dtype)


def _rope(x, pos):
    half = x.shape[-1] // 2
    inv = ROPE_THETA ** (-jnp.arange(half, dtype=jnp.float32) / half)
    ang = pos.astype(jnp.float32)[:, None] * inv[None, :]
    ang = ang.reshape((pos.shape[0],) + (1,) * (x.ndim - 3) + (half,))
    cos, sin = jnp.cos(ang), jnp.sin(ang)
    x32 = x.astype(jnp.float32)
    x1, x2 = x32[..., :half], x32[..., half:]
    return jnp.concatenate([x1 * cos - x2 * sin, x1 * sin + x2 * cos], axis=-1).astype(x.dtype)


def _expand_kv(x):
    return jnp.repeat(x, FOX_HEADS // FOX_KV_HEADS, axis=2)


def _project(x, c, pos, w_ada, b_ada, g_attn, w_in, g_qa, w_qb, g_qn, g_qr, g_kva, g_kr, g_fq, g_fk, b_f):
    bsz, t = x.shape[0], x.shape[1]
    mod = jax.nn.silu(c) @ w_ada + b_ada
    mods = jnp.split(mod[:, None, :], N_MOD, axis=-1)
    h = _rms(x, g_attn) * (1.0 + mods[1]) + mods[0]
    z = h @ w_in
    q_a, kv_lat, kv_rope, fq, fk, fv, f_logit, gate_a, gate_b = jnp.split(z, IN_OFFSETS, axis=-1)
    q = (_rms(q_a, g_qa) @ w_qb).reshape(bsz, t, MLA_HEADS, MLA_NOPE + MLA_ROPE)
    q_mla = jnp.concatenate([_rms(q[..., :MLA_NOPE], g_qn), _rope(_rms(q[..., MLA_NOPE:], g_qr), pos)], axis=-1)
    latent = _rms(kv_lat, g_kva)
    krope = _rope(_rms(kv_rope, g_kr), pos)
    q_fox = _rms(fq.reshape(bsz, t, FOX_HEADS, FOX_HEAD_DIM), g_fq)
    k_fox = _rms(fk.reshape(bsz, t, FOX_KV_HEADS, FOX_HEAD_DIM), g_fk)
    v_fox = fv.reshape(bsz, t, FOX_KV_HEADS, FOX_HEAD_DIM)
    logf = jax.nn.log_sigmoid((f_logit + b_f).astype(jnp.float32))
    return mods, q_mla, latent, krope, q_fox, k_fox, v_fox, logf, gate_a, gate_b


def _mla_kv(latent, krope, w_uk, w_uv, g_kn):
    k_nope = _rms(jnp.einsum('bsr,rhd->bshd', latent, w_uk), g_kn)
    k_rope = jnp.broadcast_to(krope[:, :, None, :], k_nope.shape[:3] + (MLA_ROPE,))
    k = jnp.concatenate([k_nope, k_rope.astype(k_nope.dtype)], axis=-1)
    v = jnp.einsum('bsr,rhd->bshd', latent, w_uv)
    return k, v


def _block_causal_attention(q, k, v, cum=None):
    bsz, s_len, nh, dk = q.shape
    scale = dk ** -0.5
    nb = s_len // Q_BLOCK
    kpos = jnp.arange(s_len)
    q_blocks = q.reshape(bsz, nb, Q_BLOCK, nh, dk).swapaxes(0, 1)
    qpos_blocks = kpos.reshape(nb, Q_BLOCK)
    cum_k = None if cum is None else cum.swapaxes(1, 2)[:, :, None, :]

    def one_block(args):
        qb, qp = args[0], args[1]
        s = jnp.einsum('bqhd,bkhd->bhqk', qb, k).astype(jnp.float32) * scale
        if cum is not None:
            s = s + (args[2].swapaxes(1, 2)[..., None] - cum_k)
        s = jnp.where(kpos[None, :] <= qp[:, None], s, -jnp.inf)
        p = jax.nn.softmax(s, axis=-1)
        return jnp.einsum('bhqk,bkhd->bqhd', p.astype(v.dtype), v)

    if cum is None:
        xs = (q_blocks, qpos_blocks)
    else:
        xs = (q_blocks, qpos_blocks, cum.reshape(bsz, nb, Q_BLOCK, nh).swapaxes(0, 1))
    out = lax.map(one_block, xs)
    return out.swapaxes(0, 1).reshape(bsz, s_len, nh, v.shape[-1])


def _init_carry(db, nh, t, dv):
    return (jnp.full((db, nh, t), -jnp.inf, jnp.float32), jnp.zeros((db, nh, t), jnp.float32),
            jnp.zeros((db, nh, t, dv), jnp.float32))


def _online_update(carry, s, v):
    m, l, acc = carry
    m_new = jnp.maximum(m, jnp.max(s, axis=-1))
    alpha = jnp.exp(m - m_new)
    p = jnp.exp(s - m_new[..., None])
    l_new = alpha * l + jnp.sum(p, axis=-1)
    acc_new = alpha[..., None] * acc + jnp.einsum('bhts,bshd->bhtd', p, v.astype(jnp.float32))
    return (m_new, l_new, acc_new)


def _paged_mla(q, latent_new, krope_new, cache_lat, cache_kr, layer, page_table, w_uk, w_uv, g_kn):
    db, t, nh, dk = q.shape
    scale = dk ** -0.5

    def step(carry, pids):
        k, v = _mla_kv(cache_lat[layer, pids], cache_kr[layer, pids], w_uk, w_uv, g_kn)
        s = jnp.einsum('bthd,bshd->bhts', q, k).astype(jnp.float32) * scale
        return _online_update(carry, s, v), None

    carry, _ = lax.scan(step, _init_carry(db, nh, t, MLA_V), page_table.T)
    k_new, v_new = _mla_kv(latent_new, krope_new, w_uk, w_uv, g_kn)
    s = jnp.einsum('bthd,bshd->bhts', q, k_new).astype(jnp.float32) * scale
    s = jnp.where(jnp.tril(jnp.ones((t, t), dtype=bool)), s, -jnp.inf)
    _, l, acc = _online_update(carry, s, v_new)
    return (acc / l[..., None]).swapaxes(1, 2).astype(q.dtype)


def _paged_fox(q, k_new, v_new, logf_new, cache_k, cache_v, cache_logf, layer, page_table):
    db, t, nh, dh = q.shape
    scale = dh ** -0.5
    n_pages = page_table.shape[1]
    logf_past = cache_logf[layer, page_table].astype(jnp.float32).reshape(db, n_pages * PAGE_SIZE, nh)
    suffix = lax.cumsum(logf_past, axis=1, reverse=True) - logf_past
    cn = jnp.cumsum(logf_new, axis=1)
    cn_t = cn.swapaxes(1, 2)[..., None]
    bias_pages = suffix.reshape(db, n_pages, PAGE_SIZE, nh).transpose(1, 0, 3, 2)

    def step(carry, xs):
        pids, bias = xs
        k = _expand_kv(cache_k[layer, pids])
        v = _expand_kv(cache_v[layer, pids])
        s = jnp.einsum('bthd,bshd->bhts', q, k).astype(jnp.float32) * scale + cn_t + bias[:, :, None, :]
        return _online_update(carry, s, v), None

    carry, _ = lax.scan(step, _init_carry(db, nh, t, dh), (page_table.T, bias_pages))
    s = jnp.einsum('bthd,bshd->bhts', q, _expand_kv(k_new)).astype(jnp.float32) * scale
    s = s + cn_t - cn.swapaxes(1, 2)[:, :, None, :]
    s = jnp.where(jnp.tril(jnp.ones((t, t), dtype=bool)), s, -jnp.inf)
    _, l, acc = _online_update(carry, s, _expand_kv(v_new))
    return (acc / l[..., None]).swapaxes(1, 2).astype(q.dtype)


def _hier_moe(h, w_rg, b_rg, w_re, b_re, w_e_gate, w_e_up, w_e_down):
    n = h.shape[0]
    p_group = jax.nn.softmax((h @ w_rg + b_rg).astype(jnp.float32), axis=-1)
    pg_top, g_idx = lax.top_k(p_group, 1)
    e_logits = (h @ w_re + b_re).astype(jnp.float32).reshape(n, N_GROUPS, EXPERTS_PER_GROUP)
    e_sel = jnp.take_along_axis(e_logits, g_idx[:, :, None], axis=1)[:, 0]
    pe_top, e_idx = lax.top_k(jax.nn.softmax(e_sel, axis=-1), TOP_K)
    weights = pg_top * pe_top / jnp.sum(pe_top, axis=-1, keepdims=True)
    expert_id = g_idx * EXPERTS_PER_GROUP + e_idx
    combine = jnp.sum(jax.nn.one_hot(expert_id, N_EXPERTS, dtype=jnp.float32) * weights[..., None], axis=1)

    def expert(y, xs):
        wg, wu, wd, ce = xs
        hid = jax.nn.silu(h @ wg) * (h @ wu)
        return y + ce[:, None].astype(h.dtype) * (hid @ wd), None

    y, _ = lax.scan(expert, jnp.zeros_like(h), (w_e_gate, w_e_up, w_e_down, combine.T))
    return y


def _merge_ffn(x, mods, o_mla, o_fox, gate_a, gate_b, w_br_a, w_br_b, w_o, g_ffn,
               w_rg, b_rg, w_re, b_re, w_e_gate, w_e_up, w_e_down):
    bsz, t = x.shape[0], x.shape[1]
    u = (jax.nn.sigmoid(gate_a) * (o_mla.reshape(bsz, t, -1) @ w_br_a)
         + jax.nn.sigmoid(gate_b) * (o_fox.reshape(bsz, t, -1) @ w_br_b))
    x1 = x + mods[2] * (u @ w_o)
    h2 = _rms(x1, g_ffn) * (1.0 + mods[4]) + mods[3]
    ffn = _hier_moe(h2.reshape(bsz * t, D_MODEL), w_rg, b_rg, w_re, b_re, w_e_gate, w_e_up, w_e_down)
    return x1 + mods[5] * ffn.reshape(bsz, t, D_MODEL)


def setup_inputs(seed: int = 0) -> dict:
    key = jax.random.key(seed)
    ks = iter(jax.random.split(key, 48))
    f32 = jnp.float32

    def nrm(shape, scale):
        return jax.random.normal(next(ks), shape, f32) * scale

    def gain(shape):
        return 1.0 + nrm(shape, 0.02)

    L = DEPTH
    n_pages = PAST_LEN // PAGE_SIZE
    n_used = DEC_BATCH * n_pages
    n_pool = n_used + n_used // 4
    page_table = jax.random.permutation(next(ks), n_pool)[:n_used].reshape(DEC_BATCH, n_pages).astype(jnp.int32)
    d = D_MODEL
    return {
        'x_prompt': nrm((BATCH, SEQ, d), 1.0),
        'x_sample': nrm((DEC_BATCH, DEC_SEQ, d), 1.0),
        'cache_mla_latent': nrm((L, n_pool, PAGE_SIZE, MLA_KV_RANK), 1.0),
        'cache_mla_krope': nrm((L, n_pool, PAGE_SIZE, MLA_ROPE), 1.0),
        'cache_fox_k': nrm((L, n_pool, PAGE_SIZE, FOX_KV_HEADS, FOX_HEAD_DIM), 1.0),
        'cache_fox_v': nrm((L, n_pool, PAGE_SIZE, FOX_KV_HEADS, FOX_HEAD_DIM), 1.0),
        'cache_fox_logf': jax.nn.log_sigmoid(FORGET_BIAS + nrm((L, n_pool, PAGE_SIZE, FOX_HEADS), 1.0)),
        'page_table': page_table,
        'c_prompt': nrm((BATCH, d), 1.0),
        'c_sample': nrm((DEC_BATCH, d), 1.0),
        'w_ada': nrm((L, d, N_MOD * d), 0.5 * d ** -0.5),
        'b_ada': nrm((L, N_MOD * d), 0.01),
        'g_attn': gain((L, d)),
        'w_in': nrm((L, d, IN_COLS), d ** -0.5),
        'g_qa': gain((L, MLA_Q_RANK)),
        'w_qb': nrm((L, MLA_Q_RANK, MLA_HEADS * (MLA_NOPE + MLA_ROPE)), MLA_Q_RANK ** -0.5),
        'g_qn': gain((L, MLA_NOPE)),
        'g_qr': gain((L, MLA_ROPE)),
        'g_kva': gain((L, MLA_KV_RANK)),
        'g_kr': gain((L, MLA_ROPE)),
        'w_uk': nrm((L, MLA_KV_RANK, MLA_HEADS, MLA_NOPE), MLA_KV_RANK ** -0.5),
        'w_uv': nrm((L, MLA_KV_RANK, MLA_HEADS, MLA_V), MLA_KV_RANK ** -0.5),
        'g_kn': gain((L, MLA_NOPE)),
        'g_fq': gain((L, FOX_HEAD_DIM)),
        'g_fk': gain((L, FOX_HEAD_DIM)),
        'b_f': FORGET_BIAS + nrm((L, FOX_HEADS), 0.5),
        'w_br_a': nrm((L, MLA_HEADS * MLA_V, d), (MLA_HEADS * MLA_V) ** -0.5),
        'w_br_b': nrm((L, FOX_HEADS * FOX_HEAD_DIM, d), (FOX_HEADS * FOX_HEAD_DIM) ** -0.5),
        'w_o': nrm((L, d, d), d ** -0.5),
        'g_ffn': gain((L, d)),
        'w_rg': nrm((L, d, N_GROUPS), d ** -0.5),
        'b_rg': nrm((L, N_GROUPS), 0.01),
        'w_re': nrm((L, d, N_EXPERTS), d ** -0.5),
        'b_re': nrm((L, N_EXPERTS), 0.01),
        'w_e_gate': nrm((L, N_EXPERTS, d, D_EXPERT), d ** -0.5),
        'w_e_up': nrm((L, N_EXPERTS, d, D_EXPERT), d ** -0.5),
        'w_e_down': nrm((L, N_EXPERTS, D_EXPERT, d), D_EXPERT ** -0.5),
    }


def reference(x_prompt, x_sample, cache_mla_latent, cache_mla_krope, cache_fox_k, cache_fox_v, cache_fox_logf,
              page_table, c_prompt, c_sample, w_ada, b_ada, g_attn, w_in, g_qa, w_qb, g_qn, g_qr, g_kva, g_kr,
              w_uk, w_uv, g_kn, g_fq, g_fk, b_f, w_br_a, w_br_b, w_o, g_ffn, w_rg, b_rg, w_re, b_re,
              w_e_gate, w_e_up, w_e_down):
    pos_p = jnp.arange(SEQ, dtype=jnp.int32)
    pos_s = PAST_LEN + jnp.arange(DEC_SEQ, dtype=jnp.int32)
    y_p, y_s = x_prompt, x_sample
    lat_p_l, kr_p_l, fk_p_l, fv_p_l, lf_p_l = [], [], [], [], []
    lat_s_l, kr_s_l, fk_s_l, fv_s_l, lf_s_l = [], [], [], [], []
    for layer in range(DEPTH):
        proj_w = tuple(a[layer] for a in (w_ada, b_ada, g_attn, w_in, g_qa, w_qb, g_qn, g_qr, g_kva, g_kr, g_fq, g_fk, b_f))
        mla_w = (w_uk[layer], w_uv[layer], g_kn[layer])
        post_w = tuple(a[layer] for a in (w_br_a, w_br_b, w_o, g_ffn, w_rg, b_rg, w_re, b_re, w_e_gate, w_e_up, w_e_down))

        mods, q, lat, kr, fq, fk, fv, logf, ga, gb = _project(y_p, c_prompt, pos_p, *proj_w)
        k_mla, v_mla = _mla_kv(lat, kr, *mla_w)
        o_mla = _block_causal_attention(q, k_mla, v_mla)
        o_fox = _block_causal_attention(fq, _expand_kv(fk), _expand_kv(fv), jnp.cumsum(logf, axis=1))
        y_p = _merge_ffn(y_p, mods, o_mla, o_fox, ga, gb, *post_w)
        lat_p_l.append(lat); kr_p_l.append(kr); fk_p_l.append(fk); fv_p_l.append(fv); lf_p_l.append(logf)

        mods, q, lat, kr, fq, fk, fv, logf, ga, gb = _project(y_s, c_sample, pos_s, *proj_w)
        o_mla = _paged_mla(q, lat, kr, cache_mla_latent, cache_mla_krope, layer, page_table, *mla_w)
        o_fox = _paged_fox(fq, fk, fv, logf, cache_fox_k, cache_fox_v, cache_fox_logf, layer, page_table)
        y_s = _merge_ffn(y_s, mods, o_mla, o_fox, ga, gb, *post_w)
        lat_s_l.append(lat); kr_s_l.append(kr); fk_s_l.append(fk); fv_s_l.append(fv); lf_s_l.append(logf)

    lat_prompt = jnp.stack(lat_p_l, axis=0)
    krope_prompt = jnp.stack(kr_p_l, axis=0)
    fk_prompt = jnp.stack(fk_p_l, axis=0)
    fv_prompt = jnp.stack(fv_p_l, axis=0)
    logf_prompt = jnp.stack(lf_p_l, axis=0)
    lat_sample = jnp.stack(lat_s_l, axis=0)
    krope_sample = jnp.stack(kr_s_l, axis=0)
    fk_sample = jnp.stack(fk_s_l, axis=0)
    fv_sample = jnp.stack(fv_s_l, axis=0)
    logf_sample = jnp.stack(lf_s_l, axis=0)
    return (y_p, y_s, lat_prompt, krope_prompt, fk_prompt, fv_prompt, logf_prompt,
            lat_sample, krope_sample, fk_sample, fv_sample, logf_sample)
```

```python
import functools

import numpy as np
import jax
import jax.numpy as jnp
from jax import lax
from jax.experimental import pallas as pl
from jax.experimental.pallas import tpu as pltpu

F32 = jnp.float32
BF16 = jnp.bfloat16

MLA_HEADS, MLA_NOPE, MLA_ROPE, MLA_V = 8, 64, 32, 64
MLA_Q_RANK, MLA_KV_RANK = 512, 256
FOX_HEADS, FOX_KV_HEADS, FOX_HEAD_DIM = 8, 4, 64
N_GROUPS, EXPERTS_PER_GROUP, TOP_K = 4, 8, 2
N_EXPERTS = N_GROUPS * EXPERTS_PER_GROUP
N_MOD = 6
ROPE_THETA = 10000.0
RMS_EPS = 1e-6
PAGE = 128
LANES = 128
AUG0 = 64
LF0 = 64

VMEM_LIMIT = 56 * 1024 * 1024
TM_PROJ = 256
TM_GATES = 512
TN_GATES = 1024
TN_MOD = 1024
TQ_ATTN = 512
PAGES_PER_CHUNK = 8
TG_MOE = 256
TM_FINAL = 512


def _cparams(sem):
    return pltpu.CompilerParams(dimension_semantics=sem, vmem_limit_bytes=VMEM_LIMIT)


def _vmem_whole():
    return pl.BlockSpec(memory_space=pltpu.VMEM)


def _dot(a, b):
    return jnp.dot(a, b, preferred_element_type=F32)


def _dot_nt(a, b):
    return lax.dot_general(a, b, (((1,), (1,)), ((), ())), preferred_element_type=F32)


def _split3(x):
    hi = x.astype(BF16)
    r1 = x - hi.astype(F32)
    mid = r1.astype(BF16)
    lo = (r1 - mid.astype(F32)).astype(BF16)
    return hi, mid, lo


def _group_rsqrt(v, e_ref, inv_cnt):
    ss = _dot((v * v).astype(BF16), e_ref[...])
    return lax.rsqrt(ss * inv_cnt + RMS_EPS)


def _group_bcast(r, et_ref):
    rh = r.astype(BF16)
    rl = (r - rh.astype(F32)).astype(BF16)
    return _dot(rh, et_ref[...]) + _dot(rl, et_ref[...])


def _mod_kernel(c_ref, w_ref, b_ref, o_ref):
    c = c_ref[...]
    a = (c / (1.0 + jnp.exp(-c))).astype(BF16)
    o_ref[...] = _dot(a, w_ref[...].astype(BF16)) + b_ref[...]


def _mod_call(c_all, w_ada, b_ada):
    rows, d = c_all.shape
    n = w_ada.shape[1]
    return pl.pallas_call(
        _mod_kernel,
        out_shape=jax.ShapeDtypeStruct((rows, n), F32),
        grid=(n // TN_MOD,),
        in_specs=[pl.BlockSpec((rows, d), lambda j: (0, 0)),
                  pl.BlockSpec((d, TN_MOD), lambda j: (0, j)),
                  pl.BlockSpec((1, TN_MOD), lambda j: (0, j))],
        out_specs=pl.BlockSpec((rows, TN_MOD), lambda j: (0, j)),
        compiler_params=_cparams(("arbitrary",)),
        name="mod",
    )(c_all, w_ada, b_ada)


C_QA = (0, 512)
C_KVL = (512, 768)
C_FQ = (768, 1792)
C_FK = (1792, 2304)
C_FV = (2304, 2560)
C_MISC = (2560, 2688)
N_SMALL = 2688


def _proj_kernel(x_ref, m0_ref, m1_ref, gattn_ref, wsm_ref, wqb_ref, wkv_ref,
                 gqa_ref, gkva_ref, gq_ref, gkr_ref, gfq_ref, gfk_ref, gkn_ref, bf_ref,
                 tab_ref, eq_ref, eqt_ref, cntq_ref, ef_ref, eft_ref, ek_ref, ekt_ref,
                 pkr_ref, pq_ref, pk_ref, onesq_ref, onesk_ref, tri_ref,
                 hb_ref, qmla_ref, lat_ref, kr_ref, qfa_ref, kfp_ref, kfa_ref, vf_ref, vfb_ref,
                 lf_ref, cum_ref, kmla_ref, vmla_ref, carry_ref, *, tiles_per_seq, with_kv):
    i = pl.program_id(0)
    x = x_ref[...]
    xn = x * lax.rsqrt(jnp.mean(x * x, axis=-1, keepdims=True) + RMS_EPS) * gattn_ref[...]
    h = xn * (1.0 + m1_ref[...]) + m0_ref[...]
    hb = h.astype(BF16)
    hb_ref[...] = hb
    z = _dot(hb, wsm_ref[...])
    qa = z[:, C_QA[0]:C_QA[1]]
    kvl = z[:, C_KVL[0]:C_KVL[1]]
    fq = z[:, C_FQ[0]:C_FQ[1]]
    fk = z[:, C_FK[0]:C_FK[1]]
    fv = z[:, C_FV[0]:C_FV[1]]
    misc = z[:, C_MISC[0]:C_MISC[1]]
    cosq, sinq, cosk, sink = tab_ref[0], tab_ref[1], tab_ref[2], tab_ref[3]

    qan = qa * lax.rsqrt(jnp.mean(qa * qa, axis=-1, keepdims=True) + RMS_EPS) * gqa_ref[...]
    qq = _dot(qan.astype(BF16), wqb_ref[...])
    nq = MLA_HEADS * LANES
    q, qsw = qq[:, :nq], qq[:, nq:]
    sc = _group_bcast(_group_rsqrt(q, eq_ref, cntq_ref[...]), eqt_ref)
    gq = gq_ref[...]
    for hh in range(MLA_HEADS):
        sl = slice(hh * LANES, (hh + 1) * LANES)
        qo = sc[:, sl] * (q[:, sl] * (gq[0:1, sl] * cosq) + qsw[:, sl] * (gq[1:2, sl] * sinq))
        qmla_ref[:, sl] = qo.astype(BF16)

    lat = kvl * lax.rsqrt(jnp.mean(kvl * kvl, axis=-1, keepdims=True) + RMS_EPS) * gkva_ref[...]
    lat_ref[...] = lat

    lane = lax.broadcasted_iota(jnp.int32, misc.shape, 1)
    kr2 = jnp.where(lane < MLA_ROPE, misc * misc, 0.0)
    rk = lax.rsqrt(jnp.sum(kr2, axis=-1, keepdims=True) * (1.0 / MLA_ROPE) + RMS_EPS)
    msw = pltpu.roll(misc, LANES - MLA_ROPE, 1)
    gkr = gkr_ref[...]
    kr = rk * (misc * (gkr[0:1] * cosk) + msw * (gkr[1:2] * sink))
    kr_ref[...] = kr

    xl = misc + bf_ref[...]
    lf = jnp.minimum(xl, 0.0) - jnp.log1p(jnp.exp(-jnp.abs(xl)))
    lf = jnp.where((lane >= LF0) & (lane < LF0 + FOX_HEADS), lf, 0.0)
    lf_ref[...] = lf
    l_hi, l_mid, l_lo = _split3(lf)
    tri = tri_ref[...]
    cum = _dot(tri, l_hi) + _dot(tri, l_mid) + _dot(tri, l_lo)
    if tiles_per_seq > 1:
        @pl.when(i % tiles_per_seq == 0)
        def _():
            carry_ref[...] = jnp.zeros_like(carry_ref)
        cum = cum + carry_ref[...]
        carry_ref[...] = cum[cum.shape[0] - 1:, :]
    cum_ref[...] = cum
    c_hi, c_mid, c_lo = _split3(cum)
    cs = jnp.concatenate([c_hi, c_mid, c_lo], axis=1)
    augq = _dot(cs, pq_ref[...]) + onesq_ref[...]
    augk = _dot(cs, pk_ref[...]) + onesk_ref[...]

    scf = _group_bcast(_group_rsqrt(fq, ef_ref, 1.0 / FOX_HEAD_DIM), eft_ref)
    qfa_ref[...] = (fq * scf * gfq_ref[...] + augq).astype(BF16)
    sck = _group_bcast(_group_rsqrt(fk, ek_ref, 1.0 / FOX_HEAD_DIM), ekt_ref)
    kf = fk * sck * gfk_ref[...]
    kfp_ref[...] = kf
    kfa_ref[...] = (kf + augk).astype(BF16)
    vf_ref[...] = fv
    vfb_ref[...] = fv.astype(BF16)

    if with_kv:
        kv = _dot(lat.astype(BF16), wkv_ref[...])
        kn, vm = kv[:, :nq], kv[:, nq:]
        scn = _group_bcast(_group_rsqrt(kn, ef_ref, 1.0 / MLA_NOPE), eft_ref)
        kp = _dot(kr.astype(BF16), pkr_ref[...])
        kmla_ref[...] = (kn * scn * gkn_ref[...] + kp).astype(BF16)
        vmla_ref[...] = vm.astype(BF16)
    else:
        kmla_ref[...] = jnp.zeros_like(kmla_ref)
        vmla_ref[...] = jnp.zeros_like(vmla_ref)


def _proj_call(x2d, m0, m1, mod_specs, tabs, tri, consts, *, seq_rows, with_kv):
    n, d = x2d.shape
    tm = min(TM_PROJ, n)
    tiles_per_seq = max(seq_rows // tm, 1)
    n_tab_tiles = tabs.shape[1] // tm
    row = lambda c: pl.BlockSpec((tm, c), lambda i: (i, 0))
    in_specs = [row(d), mod_specs[0], mod_specs[1]] + [_vmem_whole()] * 12
    in_specs += [pl.BlockSpec((4, tm, LANES), lambda i: (0, i % n_tab_tiles, 0))]
    in_specs += [_vmem_whole()] * 12
    in_specs += [pl.BlockSpec((tm, tm), lambda i: (0, 0))]
    nq = MLA_HEADS * LANES
    out_cols = [(d, BF16), (nq, BF16), (MLA_KV_RANK, F32), (LANES, F32), (FOX_HEADS * LANES, BF16),
                (FOX_KV_HEADS * LANES, F32), (FOX_KV_HEADS * LANES, BF16),
                (FOX_KV_HEADS * FOX_HEAD_DIM, F32), (FOX_KV_HEADS * FOX_HEAD_DIM, BF16),
                (LANES, F32), (LANES, F32), (nq, BF16), (MLA_HEADS * MLA_V, BF16)]
    return pl.pallas_call(
        functools.partial(_proj_kernel, tiles_per_seq=tiles_per_seq, with_kv=with_kv),
        out_shape=[jax.ShapeDtypeStruct((n, c), dt) for c, dt in out_cols],
        grid=(n // tm,),
        in_specs=in_specs,
        out_specs=[row(c) for c, _ in out_cols],
        scratch_shapes=[pltpu.VMEM((1, LANES), F32)],
        compiler_params=_cparams(("arbitrary",)),
        name="proj",
    )(x2d, m0, m1, *consts["proj_a"], tabs, *consts["proj_b"], tri)


def _gates_kernel(h_ref, w_ref, o_ref):
    z = _dot(h_ref[...], w_ref[...])
    o_ref[...] = (1.0 / (1.0 + jnp.exp(-z))).astype(BF16)


def _gates_call(hb, wg):
    n, d = hb.shape
    nc = wg.shape[1]
    tm = min(TM_GATES, n)
    return pl.pallas_call(
        _gates_kernel,
        out_shape=jax.ShapeDtypeStruct((n, nc), BF16),
        grid=(nc // TN_GATES, n // tm),
        in_specs=[pl.BlockSpec((tm, d), lambda j, i: (i, 0)),
                  pl.BlockSpec((d, TN_GATES), lambda j, i: (0, j))],
        out_specs=pl.BlockSpec((tm, TN_GATES), lambda j, i: (i, j)),
        compiler_params=_cparams(("arbitrary", "arbitrary")),
        name="gates",
    )(hb, wg)


def _attn_kernel(q_ref, k_ref, v_ref, o_ref, m_ref, l_ref, acc_ref, *, nh, rep, dv):
    i = pl.program_id(1)
    j = pl.program_id(2)

    @pl.when(j == 0)
    def _():
        m_ref[...] = jnp.full_like(m_ref, -jnp.inf)
        l_ref[...] = jnp.zeros_like(l_ref)
        acc_ref[...] = jnp.zeros_like(acc_ref)

    def update(masked):
        for h in range(nh):
            g = h // rep
            q = q_ref[0, :, h * LANES:(h + 1) * LANES]
            k = k_ref[0, :, g * LANES:(g + 1) * LANES]
            v = v_ref[0, :, g * dv:(g + 1) * dv]
            s = _dot_nt(q, k)
            if masked:
                rows = lax.broadcasted_iota(jnp.int32, s.shape, 0)
                cols = lax.broadcasted_iota(jnp.int32, s.shape, 1)
                s = jnp.where(cols <= rows, s, -jnp.inf)
            m_prev = m_ref[h]
            m_new = jnp.maximum(m_prev, jnp.max(s, axis=-1, keepdims=True))
            alpha = jnp.exp(m_prev - m_new)
            p = jnp.exp(s - m_new)
            l_ref[h] = alpha * l_ref[h] + jnp.sum(p, axis=-1, keepdims=True)
            acc_ref[h] = alpha * acc_ref[h] + _dot(p.astype(BF16), v)
            m_ref[h] = m_new

    @pl.when(j < i)
    def _():
        update(False)

    @pl.when(j == i)
    def _():
        update(True)
        for h in range(nh):
            o_ref[0, :, h * dv:(h + 1) * dv] = (acc_ref[h] / l_ref[h]).astype(BF16)


def _attn_call(q, k, v, *, nh, rep, dv, name):
    b, t, _ = q.shape
    tq = min(TQ_ATTN, t)
    nq = t // tq
    return pl.pallas_call(
        functools.partial(_attn_kernel, nh=nh, rep=rep, dv=dv),
        out_shape=jax.ShapeDtypeStruct((b, t, nh * dv), BF16),
        grid=(b, nq, nq),
        in_specs=[pl.BlockSpec((1, tq, q.shape[2]), lambda bb, i, j: (bb, i, 0)),
                  pl.BlockSpec((1, tq, k.shape[2]), lambda bb, i, j: (bb, jnp.minimum(i, j), 0)),
                  pl.BlockSpec((1, tq, v.shape[2]), lambda bb, i, j: (bb, jnp.minimum(i, j), 0))],
        out_specs=pl.BlockSpec((1, tq, nh * dv), lambda bb, i, j: (bb, i, 0)),
        scratch_shapes=[pltpu.VMEM((nh, tq, 1), F32), pltpu.VMEM((nh, tq, 1), F32),
                        pltpu.VMEM((nh, tq, dv), F32)],
        compiler_params=_cparams(("arbitrary", "arbitrary", "arbitrary")),
        name=name,
    )(q, k, v)


def _softmax_update(s, v_b, m_ref, l_ref, acc_ref):
    m_prev = m_ref[...]
    m_new = jnp.maximum(m_prev, jnp.max(s, axis=-1, keepdims=True))
    alpha = jnp.exp(m_prev - m_new)
    p = jnp.exp(s - m_new)
    l_ref[...] = alpha * l_ref[...] + jnp.sum(p, axis=-1, keepdims=True)
    acc_ref[...] = alpha * acc_ref[...] + _dot(p.astype(BF16), v_b)
    m_ref[...] = m_new


def _chunk_loop(b, n_seq, n_chunks, start, wait, compute):
    @pl.when(b == 0)
    def _():
        start(b, 0, 0)

    def pair(c2, carry):
        c = 2 * c2
        start(b, c + 1, 1)
        wait(0)
        compute(0)

        @pl.when(c + 2 < n_chunks)
        def _():
            start(b, c + 2, 0)

        @pl.when(jnp.logical_and(c + 2 >= n_chunks, b + 1 < n_seq))
        def _():
            start(b + 1, 0, 0)

        wait(1)
        compute(1)
        return carry

    lax.fori_loop(0, n_chunks // 2, pair, 0)


def _head_select(full, dv):
    rows = lax.broadcasted_iota(jnp.int32, full.shape, 0)
    cols = lax.broadcasted_iota(jnp.int32, full.shape, 1)
    nh = full.shape[1] // dv
    sel = jnp.where(cols // dv == rows % nh, full, 0.0)
    return jnp.sum(sel.reshape(full.shape[0] // nh, nh, full.shape[1]), axis=1)


def _paged_mla_kernel(pt_ref, qn_ref, qr_ref, latn_ref, krn_ref, wukt_ref, gcol_ref, wuv_ref,
                      clat_ref, ckr_ref, o_ref,
                      a_ref, wg_ref, latbuf, krbuf, newlat, newkr, sem, m_ref, l_ref, acc_ref,
                      *, n_seq, n_chunks, pages):
    b = pl.program_id(0)
    ck = pages * PAGE
    nrow = qn_ref.shape[1]
    nup = wukt_ref.shape[0]
    t_new = latn_ref.shape[1]

    def copies(seq, k, slot):
        out = []
        for p in range(pages):
            pid = pt_ref[seq, k * pages + p]
            out.append(pltpu.make_async_copy(clat_ref.at[pid], latbuf.at[slot, pl.ds(p * PAGE, PAGE)],
                                             sem.at[0, slot]))
            out.append(pltpu.make_async_copy(ckr_ref.at[pid], krbuf.at[slot, pl.ds(p * PAGE, PAGE)],
                                             sem.at[1, slot]))
        return out

    def start(seq, k, slot):
        for c in copies(seq, k, slot):
            c.start()

    def wait(slot):
        for c in copies(0, 0, slot):
            c.wait()

    @pl.when(b == 0)
    def _():
        w = wukt_ref[...]
        a_ref[0:nup, :] = w.astype(BF16)
        wg_ref[...] = (w * gcol_ref[...]).astype(BF16)
        newlat[...] = jnp.zeros_like(newlat)
        newkr[...] = jnp.zeros_like(newkr)

    a_ref[nup:nup + nrow, :] = _dot(qn_ref[0], wg_ref[...]).astype(BF16)
    m_ref[...] = jnp.full_like(m_ref, -jnp.inf)
    l_ref[...] = jnp.zeros_like(l_ref)
    acc_ref[...] = jnp.zeros_like(acc_ref)
    qr = qr_ref[0]

    def scores(lat_b, kr_b):
        r_all = _dot_nt(a_ref[...], lat_b)
        kup = r_all[0:nup]
        nk = kup.shape[1]
        ss = jnp.sum((kup * kup).reshape(MLA_HEADS, MLA_NOPE, nk), axis=1)
        rn = lax.rsqrt(ss * (1.0 / MLA_NOPE) + RMS_EPS)
        sn = r_all[nup:nup + nrow].reshape(nrow // MLA_HEADS, MLA_HEADS, nk) * rn[None]
        return sn.reshape(nrow, nk) + _dot_nt(qr, kr_b)

    def compute(slot):
        lat_b = latbuf[slot].astype(BF16)
        s = scores(lat_b, krbuf[slot].astype(BF16))
        _softmax_update(s, lat_b, m_ref, l_ref, acc_ref)

    _chunk_loop(b, n_seq, n_chunks, start, wait, compute)

    newlat[0:t_new, :] = latn_ref[0]
    newkr[0:t_new, :] = krn_ref[0]
    lat_b = newlat[...].astype(BF16)
    s = scores(lat_b, newkr[...].astype(BF16))
    rows = lax.broadcasted_iota(jnp.int32, s.shape, 0)
    cols = lax.broadcasted_iota(jnp.int32, s.shape, 1)
    s = jnp.where(cols <= rows // MLA_HEADS, s, -jnp.inf)
    _softmax_update(s, lat_b, m_ref, l_ref, acc_ref)

    o_lat = (acc_ref[...] / l_ref[...]).astype(BF16)
    o_ref[0] = _head_select(_dot(o_lat, wuv_ref[...]), MLA_V).astype(BF16)


def _paged_mla_call(pt, qn_bd, qr, lat_new, kr_new, wukt, gcol, wuv, clat, ckr):
    n_seq, n_pages = pt.shape
    pages = min(PAGES_PER_CHUNK, n_pages // 2)
    n_chunks = n_pages // pages
    ck = pages * PAGE
    nrow = qn_bd.shape[1]
    t_new = lat_new.shape[1]
    nup = wukt.shape[0]
    seq3 = lambda a: pl.BlockSpec((1,) + a.shape[1:], lambda b, pt_: (b, 0, 0))
    anyspec = pl.BlockSpec(memory_space=pl.ANY)
    grid_spec = pltpu.PrefetchScalarGridSpec(
        num_scalar_prefetch=1, grid=(n_seq,),
        in_specs=[seq3(qn_bd), seq3(qr), seq3(lat_new), seq3(kr_new),
                  _vmem_whole(), _vmem_whole(), _vmem_whole(), anyspec, anyspec],
        out_specs=pl.BlockSpec((1, t_new, MLA_HEADS * MLA_V), lambda b, pt_: (b, 0, 0)),
        scratch_shapes=[pltpu.VMEM((nup + nrow, MLA_KV_RANK), BF16),
                        pltpu.VMEM((nup, MLA_KV_RANK), BF16),
                        pltpu.VMEM((2, ck, MLA_KV_RANK), F32),
                        pltpu.VMEM((2, ck, MLA_ROPE), F32),
                        pltpu.VMEM((LANES, MLA_KV_RANK), F32),
                        pltpu.VMEM((LANES, MLA_ROPE), F32),
                        pltpu.SemaphoreType.DMA((2, 2)),
                        pltpu.VMEM((nrow, 1), F32), pltpu.VMEM((nrow, 1), F32),
                        pltpu.VMEM((nrow, MLA_KV_RANK), F32)])
    return pl.pallas_call(
        functools.partial(_paged_mla_kernel, n_seq=n_seq, n_chunks=n_chunks, pages=pages),
        out_shape=jax.ShapeDtypeStruct((n_seq, t_new, MLA_HEADS * MLA_V), BF16),
        grid_spec=grid_spec,
        compiler_params=_cparams(("arbitrary",)),
        name="paged_mla",
    )(pt, qn_bd, qr, lat_new, kr_new, wukt, gcol, wuv, clat, ckr)


def _paged_fox_kernel(pt_ref, qbd_ref, kn_ref, vn_ref, cncol_ref, cnt_ref, psel_ref,
                      ck_ref, cv_ref, clf_ref, o_ref,
                      kbuf, vbuf, lfbuf, newk, newv, sem, m_ref, l_ref, acc_ref, sufc_ref,
                      *, n_seq, n_chunks, pages):
    b = pl.program_id(0)
    ck = pages * PAGE
    nrow = qbd_ref.shape[1]
    t_new = kn_ref.shape[1]
    nt = nrow // FOX_HEADS

    def copies(seq, k, slot):
        out = []
        kk = n_chunks - 1 - k
        for p in range(pages):
            pid = pt_ref[seq, kk * pages + p]
            out.append(pltpu.make_async_copy(ck_ref.at[pid], kbuf.at[slot, pl.ds(p * PAGE, PAGE)],
                                             sem.at[0, slot]))
            out.append(pltpu.make_async_copy(cv_ref.at[pid], vbuf.at[slot, pl.ds(p * PAGE, PAGE)],
                                             sem.at[1, slot]))
            out.append(pltpu.make_async_copy(clf_ref.at[pid], lfbuf.at[slot, :, pl.ds(p * PAGE, PAGE)],
                                             sem.at[2, slot]))
        return out

    def start(seq, k, slot):
        for c in copies(seq, k, slot):
            c.start()

    def wait(slot):
        for c in copies(0, 0, slot):
            c.wait()

    @pl.when(b == 0)
    def _():
        newk[...] = jnp.zeros_like(newk)
        newv[...] = jnp.zeros_like(newv)

    m_ref[...] = jnp.full_like(m_ref, -jnp.inf)
    l_ref[...] = jnp.zeros_like(l_ref)
    acc_ref[...] = jnp.zeros_like(acc_ref)
    sufc_ref[...] = jnp.zeros_like(sufc_ref)
    qbd = qbd_ref[0]
    cncol = cncol_ref[0]
    lane = lax.broadcasted_iota(jnp.int32, (FOX_HEADS, PAGE), 1)

    def compute(slot):
        k_b = kbuf[slot].astype(BF16)
        s = _dot_nt(qbd, k_b)
        lf = lfbuf[slot]
        carry = sufc_ref[...]
        pieces = [None] * pages
        for p in reversed(range(pages)):
            xp = lf[:, p * PAGE:(p + 1) * PAGE]
            y = xp
            d = 1
            while d < PAGE:
                y = y + jnp.where(lane < PAGE - d, pltpu.roll(y, PAGE - d, 1), 0.0)
                d *= 2
            pieces[p] = (y - xp) + carry
            carry = carry + y[:, 0:1]
        sufc_ref[...] = carry
        bias = jnp.concatenate(pieces, axis=1)
        s = (s.reshape(nt, FOX_HEADS, ck) + bias[None]).reshape(nrow, ck) + cncol
        _softmax_update(s, vbuf[slot].astype(BF16), m_ref, l_ref, acc_ref)

    _chunk_loop(b, n_seq, n_chunks, start, wait, compute)

    newk[0:t_new, :] = kn_ref[0]
    newv[0:t_new, :] = vn_ref[0]
    s = _dot_nt(qbd, newk[...].astype(BF16))
    s = (s.reshape(nt, FOX_HEADS, LANES) - cnt_ref[0][None]).reshape(nrow, LANES) + cncol
    rows = lax.broadcasted_iota(jnp.int32, s.shape, 0)
    cols = lax.broadcasted_iota(jnp.int32, s.shape, 1)
    s = jnp.where(cols <= rows // FOX_HEADS, s, -jnp.inf)
    _softmax_update(s, newv[...].astype(BF16), m_ref, l_ref, acc_ref)

    o = (acc_ref[...] / l_ref[...]).astype(BF16)
    o_ref[0] = _head_select(_dot(o, psel_ref[...]), FOX_HEAD_DIM).astype(BF16)


def _paged_fox_call(pt, qbd, k_new, v_new, cn_col, cn_t, psel, cck, ccv, clft):
    n_seq, n_pages = pt.shape
    pages = min(PAGES_PER_CHUNK, n_pages // 2)
    n_chunks = n_pages // pages
    ck = pages * PAGE
    nrow = qbd.shape[1]
    t_new = k_new.shape[1]
    kvw = FOX_KV_HEADS * FOX_HEAD_DIM
    seq3 = lambda a: pl.BlockSpec((1,) + a.shape[1:], lambda b, pt_: (b, 0, 0))
    anyspec = pl.BlockSpec(memory_space=pl.ANY)
    grid_spec = pltpu.PrefetchScalarGridSpec(
        num_scalar_prefetch=1, grid=(n_seq,),
        in_specs=[seq3(qbd), seq3(k_new), seq3(v_new), seq3(cn_col), seq3(cn_t), _vmem_whole(),
                  anyspec, anyspec, anyspec],
        out_specs=pl.BlockSpec((1, t_new, FOX_HEADS * FOX_HEAD_DIM), lambda b, pt_: (b, 0, 0)),
        scratch_shapes=[pltpu.VMEM((2, ck, kvw), F32), pltpu.VMEM((2, ck, kvw), F32),
                        pltpu.VMEM((2, FOX_HEADS, ck), F32),
                        pltpu.VMEM((LANES, kvw), F32), pltpu.VMEM((LANES, kvw), F32),
                        pltpu.SemaphoreType.DMA((3, 2)),
                        pltpu.VMEM((nrow, 1), F32), pltpu.VMEM((nrow, 1), F32),
                        pltpu.VMEM((nrow, kvw), F32), pltpu.VMEM((FOX_HEADS, 1), F32)])
    return pl.pallas_call(
        functools.partial(_paged_fox_kernel, n_seq=n_seq, n_chunks=n_chunks, pages=pages),
        out_shape=jax.ShapeDtypeStruct((n_seq, t_new, FOX_HEADS * FOX_HEAD_DIM), BF16),
        grid_spec=grid_spec,
        compiler_params=_cparams(("arbitrary",)),
        name="paged_fox",
    )(pt, qbd, k_new, v_new, cn_col, cn_t, psel, cck, ccv, clft)


def _merge_kernel(x_ref, om_ref, of_ref, sg_ref, m2_ref, m3_ref, m4_ref, gffn_ref,
                  wba_ref, wbb_ref, wo_ref, wrh_ref, wrl_ref, br_ref,
                  x1_ref, h2_ref, rt_ref):
    d = x_ref.shape[1]
    a = _dot(om_ref[...], wba_ref[...])
    bb = _dot(of_ref[...], wbb_ref[...])
    u = sg_ref[:, 0:d].astype(F32) * a + sg_ref[:, d:2 * d].astype(F32) * bb
    x1 = x_ref[...] + m2_ref[...] * _dot(u.astype(BF16), wo_ref[...])
    x1_ref[...] = x1
    xn = x1 * lax.rsqrt(jnp.mean(x1 * x1, axis=-1, keepdims=True) + RMS_EPS) * gffn_ref[...]
    h2 = xn * (1.0 + m4_ref[...]) + m3_ref[...]
    h2_hi = h2.astype(BF16)
    h2_ref[...] = h2_hi
    h2_lo = (h2 - h2_hi.astype(F32)).astype(BF16)
    lg = (_dot(h2_hi, wrh_ref[...]) + _dot(h2_hi, wrl_ref[...]) + _dot(h2_lo, wrh_ref[...])
          + br_ref[...])
    lane = lax.broadcasted_iota(jnp.int32, lg.shape, 1)
    neg = -jnp.inf
    is_g = (lane >= N_EXPERTS) & (lane < N_EXPERTS + N_GROUPS)
    lgm = jnp.where(is_g, lg, neg)
    mg = jnp.max(lgm, axis=-1, keepdims=True)
    eg = jnp.exp(lgm - mg)
    pgrp = eg / jnp.sum(eg, axis=-1, keepdims=True)
    pg_top = jnp.max(pgrp, axis=-1, keepdims=True)
    g_idx = jnp.min(jnp.where(is_g & (pgrp == pg_top), lane - N_EXPERTS, N_GROUPS), axis=-1, keepdims=True)
    sel = (lane < N_EXPERTS) & (lane // EXPERTS_PER_GROUP == g_idx)
    em = jnp.where(sel, lg, neg)
    me = jnp.max(em, axis=-1, keepdims=True)
    ee = jnp.exp(em - me)
    pe = ee / jnp.sum(ee, axis=-1, keepdims=True)
    big = 4 * LANES
    p1 = jnp.max(jnp.where(sel, pe, -1.0), axis=-1, keepdims=True)
    i1 = jnp.min(jnp.where(sel & (pe == p1), lane, big), axis=-1, keepdims=True)
    sel2 = sel & (lane != i1)
    p2 = jnp.max(jnp.where(sel2, pe, -1.0), axis=-1, keepdims=True)
    i2 = jnp.min(jnp.where(sel2 & (pe == p2), lane, big), axis=-1, keepdims=True)
    den = p1 + p2
    w1 = pg_top * p1 / den
    w2 = pg_top * p2 / den
    rt = jnp.where(lane == 0, i1.astype(F32),
                   jnp.where(lane == 1, i2.astype(F32),
                             jnp.where(lane == 2, w1, jnp.where(lane == 3, w2, 0.0))))
    rt_ref[...] = rt


def _merge_call(x2d, o_mla, o_fox, sg, m2, m3, m4, mod_specs, consts):
    n, d = x2d.shape
    tm = min(TM_PROJ, n)
    row = lambda c: pl.BlockSpec((tm, c), lambda i: (i, 0))
    in_specs = [row(d), row(o_mla.shape[1]), row(o_fox.shape[1]), row(sg.shape[1]),
                mod_specs[0], mod_specs[1], mod_specs[2]] + [_vmem_whole()] * 7
    return pl.pallas_call(
        _merge_kernel,
        out_shape=[jax.ShapeDtypeStruct((n, d), F32), jax.ShapeDtypeStruct((n, d), BF16),
                   jax.ShapeDtypeStruct((n, LANES), F32)],
        grid=(n // tm,),
        in_specs=in_specs,
        out_specs=[row(d), row(d), row(LANES)],
        compiler_params=_cparams(("arbitrary",)),
        name="merge",
    )(x2d, o_mla, o_fox, sg, m2, m3, m4, *consts["merge"])


def _gmm_kernel(te_ref, nv_ref, x_ref, wg_ref, wu_ref, wd_ref, o_ref, wgb, wub, wdb):
    i = pl.program_id(0)
    valid = i < nv_ref[0]
    prev = te_ref[jnp.maximum(i - 1, 0)]
    changed = jnp.logical_or(i == 0, te_ref[i] != prev)

    @pl.when(jnp.logical_and(valid, changed))
    def _():
        wgb[...] = wg_ref[0].astype(BF16)
        wub[...] = wu_ref[0].astype(BF16)
        wdb[...] = wd_ref[0].astype(BF16)

    @pl.when(valid)
    def _():
        x = x_ref[...]
        g = _dot(x, wgb[...])
        u = _dot(x, wub[...])
        hid = (g / (1.0 + jnp.exp(-g))) * u
        o_ref[...] = _dot(hid.astype(BF16), wdb[...]).astype(BF16)

    @pl.when(jnp.logical_not(valid))
    def _():
        o_ref[...] = jnp.zeros_like(o_ref)


def _gmm_call(tile_e, n_valid, xs, w_gate, w_up, w_down):
    p_total, d = xs.shape
    de = w_gate.shape[2]
    n_tiles = p_total // TG_MOE
    grid_spec = pltpu.PrefetchScalarGridSpec(
        num_scalar_prefetch=2, grid=(n_tiles,),
        in_specs=[pl.BlockSpec((TG_MOE, d), lambda i, te, nv: (jnp.minimum(i, nv[0] - 1), 0)),
                  pl.BlockSpec((1, d, de), lambda i, te, nv: (te[i], 0, 0)),
                  pl.BlockSpec((1, d, de), lambda i, te, nv: (te[i], 0, 0)),
                  pl.BlockSpec((1, de, d), lambda i, te, nv: (te[i], 0, 0))],
        out_specs=pl.BlockSpec((TG_MOE, d), lambda i, te, nv: (i, 0)),
        scratch_shapes=[pltpu.VMEM((d, de), BF16), pltpu.VMEM((d, de), BF16), pltpu.VMEM((de, d), BF16)])
    return pl.pallas_call(
        _gmm_kernel,
        out_shape=jax.ShapeDtypeStruct((p_total, d), BF16),
        grid_spec=grid_spec,
        compiler_params=_cparams(("arbitrary",)),
        name="gmm",
    )(tile_e, n_valid, xs, w_gate, w_up, w_down)


def _final_kernel(x1_ref, ya_ref, yb_ref, rt_ref, m5_ref, o_ref):
    rt = rt_ref[...]
    ffn = rt[:, 2:3] * ya_ref[...].astype(F32) + rt[:, 3:4] * yb_ref[...].astype(F32)
    o_ref[...] = x1_ref[...] + m5_ref[...] * ffn


def _final_call(x1, ya, yb, rt, m5, m5_spec_fn):
    n, d = x1.shape
    tm = min(TM_FINAL, n)
    row = lambda c: pl.BlockSpec((tm, c), lambda i: (i, 0))
    return pl.pallas_call(
        _final_kernel,
        out_shape=jax.ShapeDtypeStruct((n, d), F32),
        grid=(n // tm,),
        in_specs=[row(d), row(d), row(d), row(LANES), m5_spec_fn(tm)],
        out_specs=row(d),
        compiler_params=_cparams(("arbitrary",)),
        name="final",
    )(x1, ya, yb, rt, m5)


def _np_indicator(n_rows, groups):
    e = np.zeros((n_rows, LANES), np.float32)
    for col, (lo, hi) in enumerate(groups):
        e[lo:hi, col] = 1.0
    return e


def _static_consts():
    c = {}
    nq = MLA_HEADS * LANES
    eq = _np_indicator(nq, [(h * LANES, h * LANES + MLA_NOPE) for h in range(MLA_HEADS)]
                       + [(h * LANES + MLA_NOPE, h * LANES + MLA_NOPE + MLA_ROPE) for h in range(MLA_HEADS)])
    cnt = np.ones((1, LANES), np.float32)
    cnt[0, :MLA_HEADS] = 1.0 / MLA_NOPE
    cnt[0, MLA_HEADS:2 * MLA_HEADS] = 1.0 / MLA_ROPE
    ef = _np_indicator(nq, [(h * LANES, h * LANES + FOX_HEAD_DIM) for h in range(FOX_HEADS)])
    ek = _np_indicator(FOX_KV_HEADS * LANES, [(g * LANES, g * LANES + FOX_HEAD_DIM) for g in range(FOX_KV_HEADS)])
    pkr = np.zeros((LANES, nq), np.float32)
    for h in range(MLA_HEADS):
        for j in range(MLA_ROPE):
            pkr[j, h * LANES + MLA_NOPE + j] = 1.0
    pq = np.zeros((3 * LANES, FOX_HEADS * LANES), np.float32)
    pk = np.zeros((3 * LANES, FOX_KV_HEADS * LANES), np.float32)
    onesq = np.zeros((1, FOX_HEADS * LANES), np.float32)
    onesk = np.zeros((1, FOX_KV_HEADS * LANES), np.float32)
    rep = FOX_HEADS // FOX_KV_HEADS
    for h in range(FOX_HEADS):
        g, e = h // rep, h % rep
        for piece in range(3):
            pq[piece * LANES + LF0 + h, h * LANES + AUG0 + piece] = 1.0
            pk[piece * LANES + LF0 + h, g * LANES + AUG0 + 3 + 3 * e + piece] = -1.0
            onesq[0, h * LANES + AUG0 + 3 + 3 * e + piece] = 1.0
    for g in range(FOX_KV_HEADS):
        onesk[0, g * LANES + AUG0:g * LANES + AUG0 + 3] = 1.0
    psel = np.zeros((FOX_KV_HEADS * FOX_HEAD_DIM, FOX_HEADS * FOX_HEAD_DIM), np.float32)
    for h in range(FOX_HEADS):
        g = h // rep
        for dd in range(FOX_HEAD_DIM):
            psel[g * FOX_HEAD_DIM + dd, h * FOX_HEAD_DIM + dd] = 1.0
    b = lambda a: jnp.asarray(a, BF16)
    c.update(eq=b(eq), eqt=b(eq.T), cntq=jnp.asarray(cnt), ef=b(ef), eft=b(ef.T), ek=b(ek), ekt=b(ek.T),
             pkr=b(pkr), pq=b(pq), pk=b(pk), onesq=jnp.asarray(onesq), onesk=jnp.asarray(onesk), psel=b(psel))
    return c


def _pad_heads(w, nh, dh):
    k = w.shape[0]
    return jnp.pad(w.reshape(k, nh, dh), ((0, 0), (0, 0), (0, LANES - dh))).reshape(k, nh * LANES)


def _head_row(g, nh, scale=1.0):
    dh = g.shape[0]
    return jnp.tile(jnp.pad(g * scale, (0, LANES - dh)), nh)[None, :]


def _rope_tables(pos):
    half = MLA_ROPE // 2
    inv = ROPE_THETA ** (-jnp.arange(half, dtype=F32) / half)
    ang = pos.astype(F32)[:, None] * inv[None, :]
    cos, sin = jnp.cos(ang), jnp.sin(ang)
    t = pos.shape[0]
    z = lambda n: jnp.zeros((t, n), F32)
    cosq = jnp.concatenate([jnp.ones((t, MLA_NOPE), F32), cos, cos, z(LANES - MLA_NOPE - MLA_ROPE)], axis=1)
    sinq = jnp.concatenate([z(MLA_NOPE), -sin, sin, z(LANES - MLA_NOPE - MLA_ROPE)], axis=1)
    cosk = jnp.concatenate([cos, cos, z(LANES - MLA_ROPE)], axis=1)
    sink = jnp.concatenate([-sin, sin, z(LANES - MLA_ROPE)], axis=1)
    return jnp.stack([cosq, sinq, cosk, sink], axis=0)


def kernel(x_prompt, x_sample, cache_mla_latent, cache_mla_krope, cache_fox_k, cache_fox_v, cache_fox_logf,
           page_table, c_prompt, c_sample, w_ada, b_ada, g_attn, w_in, g_qa, w_qb, g_qn, g_qr, g_kva, g_kr,
           w_uk, w_uv, g_kn, g_fq, g_fk, b_f, w_br_a, w_br_b, w_o, g_ffn, w_rg, b_rg, w_re, b_re,
           w_e_gate, w_e_up, w_e_down):
    depth = w_ada.shape[0]
    assert depth == 1, "single-layer trunk"
    bsz, seq, d = x_prompt.shape
    dbsz, dseq, _ = x_sample.shape
    n_pool = cache_mla_latent.shape[1]
    past = page_table.shape[1] * PAGE
    n_p, n_s = bsz * seq, dbsz * dseq
    n_tok = n_p + n_s
    sc = _static_consts()
    half = MLA_ROPE // 2
    scale_mla = float((MLA_NOPE + MLA_ROPE) ** -0.5)
    scale_fox = float(FOX_HEAD_DIM ** -0.5)

    n_c = bsz + dbsz
    n_c_pad = -(-n_c // 8) * 8
    c_all = jnp.pad(jnp.concatenate([c_prompt, c_sample], axis=0), ((0, n_c_pad - n_c), (0, 0)))
    mod = _mod_call(c_all, w_ada[0], b_ada[0][None, :])
    mod3 = mod.reshape(n_c_pad, 1, N_MOD * d)
    mod_s = jnp.repeat(mod[bsz:bsz + dbsz], dseq, axis=0)

    def mod_spec_p(k, tm):
        return pl.BlockSpec((None, 1, d), lambda i: ((i * tm) // seq, 0, k))

    def mod_spec_s(k, tm):
        return pl.BlockSpec((tm, d), lambda i: (i, k))

    qa_w, kvl_w, kr_w, fq_w, fk_w, fv_w, fl_w, ga_w, gb_w = jnp.split(
        w_in[0], np.cumsum([MLA_Q_RANK, MLA_KV_RANK, MLA_ROPE, FOX_HEADS * FOX_HEAD_DIM,
                            FOX_KV_HEADS * FOX_HEAD_DIM, FOX_KV_HEADS * FOX_HEAD_DIM, FOX_HEADS, d]).tolist(), axis=1)
    kr_sw = jnp.concatenate([kr_w[:, half:], kr_w[:, :half]], axis=1)
    misc_w = jnp.concatenate([kr_w, kr_sw, fl_w, jnp.zeros((d, LANES - 2 * MLA_ROPE - FOX_HEADS), F32)], axis=1)
    w_small = jnp.concatenate([qa_w, kvl_w, _pad_heads(fq_w, FOX_HEADS, FOX_HEAD_DIM),
                               _pad_heads(fk_w, FOX_KV_HEADS, FOX_HEAD_DIM), fv_w, misc_w], axis=1).astype(BF16)
    w_gates = jnp.concatenate([ga_w, gb_w], axis=1).astype(BF16)
    dq = MLA_NOPE + MLA_ROPE
    wqb3 = w_qb[0].reshape(MLA_Q_RANK, MLA_HEADS, dq)
    wqb_p = jnp.pad(wqb3, ((0, 0), (0, 0), (0, LANES - dq))).reshape(MLA_Q_RANK, MLA_HEADS * LANES)
    wqb_sw3 = jnp.concatenate([jnp.zeros((MLA_Q_RANK, MLA_HEADS, MLA_NOPE), F32),
                               wqb3[:, :, MLA_NOPE + half:], wqb3[:, :, MLA_NOPE:MLA_NOPE + half],
                               jnp.zeros((MLA_Q_RANK, MLA_HEADS, LANES - dq), F32)], axis=2)
    wqb_cat = jnp.concatenate([wqb_p, wqb_sw3.reshape(MLA_Q_RANK, MLA_HEADS * LANES)], axis=1).astype(BF16)
    wuk2 = w_uk[0].reshape(MLA_KV_RANK, MLA_HEADS * MLA_NOPE)
    wuv2 = w_uv[0].reshape(MLA_KV_RANK, MLA_HEADS * MLA_V)
    wkv = jnp.concatenate([_pad_heads(wuk2, MLA_HEADS, MLA_NOPE), wuv2], axis=1).astype(BF16)

    gqr = g_qr[0]
    gq_a = jnp.concatenate([g_qn[0], gqr, jnp.zeros((LANES - dq,), F32)])
    gq_b = jnp.concatenate([jnp.zeros((MLA_NOPE,), F32), gqr[half:], gqr[:half], jnp.zeros((LANES - dq,), F32)])
    gq = jnp.stack([jnp.tile(gq_a, MLA_HEADS), jnp.tile(gq_b, MLA_HEADS)], axis=0) * scale_mla
    gkr0 = g_kr[0]
    gkr = jnp.stack([jnp.pad(gkr0, (0, LANES - MLA_ROPE)),
                     jnp.pad(jnp.concatenate([gkr0[half:], gkr0[:half]]), (0, LANES - MLA_ROPE))], axis=0)
    bf_row = jnp.zeros((1, LANES), F32).at[0, LF0:LF0 + FOX_HEADS].set(b_f[0])
    proj_a = [g_attn, w_small, wqb_cat, wkv, g_qa, g_kva, gq, gkr,
              _head_row(g_fq[0], FOX_HEADS, scale_fox), _head_row(g_fk[0], FOX_KV_HEADS),
              _head_row(g_kn[0], MLA_HEADS), bf_row]
    proj_b = [sc["eq"], sc["eqt"], sc["cntq"], sc["ef"], sc["eft"], sc["ek"], sc["ekt"],
              sc["pkr"], sc["pq"], sc["pk"], sc["onesq"], sc["onesk"]]
    w_r = jnp.concatenate([w_re[0], w_rg[0], jnp.zeros((d, LANES - N_EXPERTS - N_GROUPS), F32)], axis=1)
    w_r_hi = w_r.astype(BF16)
    w_r_lo = (w_r - w_r_hi.astype(F32)).astype(BF16)
    b_r = jnp.concatenate([b_re[0], b_rg[0], jnp.zeros((LANES - N_EXPERTS - N_GROUPS,), F32)])[None, :]
    consts = {"proj_a": proj_a, "proj_b": proj_b,
              "merge": [g_ffn, w_br_a[0].astype(BF16), w_br_b[0].astype(BF16), w_o[0].astype(BF16),
                        w_r_hi, w_r_lo, b_r]}

    tm_p = min(TM_PROJ, n_p)
    tm_s = min(TM_PROJ, n_s)
    tabs_p = _rope_tables(jnp.arange(seq, dtype=jnp.int32))
    tabs_s = _rope_tables(past + (jnp.arange(tm_s, dtype=jnp.int32) % dseq))
    r_p = np.arange(tm_p)
    tri_p = jnp.asarray(r_p[None, :] <= r_p[:, None], BF16)
    r_s = np.arange(tm_s)
    tri_s = jnp.asarray((r_s[None, :] <= r_s[:, None]) & (r_s[None, :] // dseq == r_s[:, None] // dseq), BF16)
    xp2 = x_prompt.reshape(n_p, d)
    xs2 = x_sample.reshape(n_s, d)
    (hb_p, qmla_p, lat_p, kr_p, qfa_p, kfp_p, kfa_p, vf_p, vfb_p, lf_p, cum_p, kmla_p, vmla_p) = _proj_call(
        xp2, mod3, mod3, (mod_spec_p(0, tm_p), mod_spec_p(1, tm_p)), tabs_p, tri_p, consts,
        seq_rows=seq, with_kv=True)
    (hb_s, qmla_s, lat_s, kr_s, qfa_s, kfp_s, kfa_s, vf_s, vfb_s, lf_s, cum_s, _, _) = _proj_call(
        xs2, mod_s, mod_s, (mod_spec_s(0, tm_s), mod_spec_s(1, tm_s)), tabs_s, tri_s, consts,
        seq_rows=tm_s, with_kv=False)

    sg_p = _gates_call(hb_p, w_gates)
    sg_s = _gates_call(hb_s, w_gates)

    o_mla_p = _attn_call(qmla_p.reshape(bsz, seq, -1), kmla_p.reshape(bsz, seq, -1), vmla_p.reshape(bsz, seq, -1),
                         nh=MLA_HEADS, rep=1, dv=MLA_V, name="attn_mla").reshape(n_p, -1)
    o_fox_p = _attn_call(qfa_p.reshape(bsz, seq, -1), kfa_p.reshape(bsz, seq, -1), vfb_p.reshape(bsz, seq, -1),
                         nh=FOX_HEADS, rep=FOX_HEADS // FOX_KV_HEADS, dv=FOX_HEAD_DIM,
                         name="attn_fox").reshape(n_p, -1)

    nrow = dseq * MLA_HEADS
    q4 = qmla_s.reshape(dbsz, dseq, MLA_HEADS, LANES)
    eye_h = jnp.eye(MLA_HEADS, dtype=BF16)
    qn_bd = (q4[:, :, :, None, :MLA_NOPE] * eye_h[None, None, :, :, None]).reshape(dbsz, nrow, MLA_HEADS * MLA_NOPE)
    qr = q4[..., MLA_NOPE:dq].reshape(dbsz, nrow, MLA_ROPE)
    wukt = wuk2.T
    gcol = jnp.tile(g_kn[0], MLA_HEADS)[:, None]
    o_mla_s = _paged_mla_call(page_table, qn_bd, qr, lat_s.reshape(dbsz, dseq, -1),
                              kr_s[:, :MLA_ROPE].reshape(dbsz, dseq, MLA_ROPE),
                              wukt, gcol, wuv2.astype(BF16),
                              cache_mla_latent[0], cache_mla_krope[0]).reshape(n_s, -1)

    qf4 = qfa_s.reshape(dbsz, dseq, FOX_HEADS, LANES)[..., :FOX_HEAD_DIM]
    rep = FOX_HEADS // FOX_KV_HEADS
    onehot_g = jnp.asarray(np.arange(FOX_HEADS)[:, None] // rep == np.arange(FOX_KV_HEADS)[None, :], BF16)
    qbd = (qf4[:, :, :, None, :] * onehot_g[None, None, :, :, None]).reshape(dbsz, nrow, FOX_KV_HEADS * FOX_HEAD_DIM)
    kf_s = kfp_s.reshape(n_s, FOX_KV_HEADS, LANES)[:, :, :FOX_HEAD_DIM]
    cn = cum_s[:, LF0:LF0 + FOX_HEADS].reshape(dbsz, dseq, FOX_HEADS)
    cn_col = cn.reshape(dbsz, nrow, 1)
    cn_t = jnp.pad(cn.transpose(0, 2, 1), ((0, 0), (0, 0), (0, LANES - dseq)))
    kvw = FOX_KV_HEADS * FOX_HEAD_DIM
    clft = cache_fox_logf[0].transpose(0, 2, 1)
    o_fox_s = _paged_fox_call(page_table, qbd, kf_s.reshape(dbsz, dseq, kvw), vf_s.reshape(dbsz, dseq, kvw),
                              cn_col, cn_t, sc["psel"],
                              cache_fox_k[0].reshape(n_pool, PAGE, kvw), cache_fox_v[0].reshape(n_pool, PAGE, kvw),
                              clft).reshape(n_s, -1)

    x1_p, h2_p, rt_p = _merge_call(xp2, o_mla_p, o_fox_p, sg_p, mod3, mod3, mod3,
                                   (mod_spec_p(2, tm_p), mod_spec_p(3, tm_p), mod_spec_p(4, tm_p)), consts)
    x1_s, h2_s, rt_s = _merge_call(xs2, o_mla_s, o_fox_s, sg_s, mod_s, mod_s, mod_s,
                                   (mod_spec_s(2, tm_s), mod_spec_s(3, tm_s), mod_spec_s(4, tm_s)), consts)

    h2 = jnp.concatenate([h2_p, h2_s], axis=0)
    rt = jnp.concatenate([rt_p, rt_s], axis=0)
    e_flat = rt[:, 0:TOP_K].astype(jnp.int32).reshape(-1)
    n_pairs = n_tok * TOP_K
    counts = jnp.zeros((N_EXPERTS,), jnp.int32).at[e_flat].add(1)
    tiles_e = (counts + TG_MOE - 1) // TG_MOE
    tile_end = jnp.cumsum(tiles_e)
    tile_start = tile_end - tiles_e
    grp_start = jnp.cumsum(counts) - counts
    order = jnp.argsort(e_flat, stable=True)
    e_sorted = e_flat[order]
    dest_sorted = tile_start[e_sorted] * TG_MOE + (jnp.arange(n_pairs, dtype=jnp.int32) - grp_start[e_sorted])
    n_tiles = n_pairs // TG_MOE + N_EXPERTS
    p_total = n_tiles * TG_MOE
    src_tok = jnp.zeros((p_total,), jnp.int32).at[dest_sorted].set((order // TOP_K).astype(jnp.int32))
    dest = jnp.zeros((n_pairs,), jnp.int32).at[order].set(dest_sorted)
    n_valid = tile_end[-1:].astype(jnp.int32)
    tile_ids = jnp.arange(n_tiles, dtype=jnp.int32)
    tile_e = jnp.minimum(jnp.searchsorted(tile_end, tile_ids, side="right"), N_EXPERTS - 1).astype(jnp.int32)
    tile_e = jnp.where(tile_ids < n_valid[0], tile_e, tile_e[jnp.maximum(n_valid[0] - 1, 0)])
    xs_sorted = jnp.take(h2, src_tok, axis=0)
    ys = _gmm_call(tile_e, n_valid, xs_sorted, w_e_gate[0], w_e_up[0], w_e_down[0])
    dest2 = dest.reshape(n_tok, TOP_K)
    ya = jnp.take(ys, dest2[:, 0], axis=0)
    yb = jnp.take(ys, dest2[:, 1], axis=0)

    y_p = _final_call(x1_p, ya[:n_p], yb[:n_p], rt_p, mod3,
                      lambda tm: pl.BlockSpec((None, 1, d), lambda i: ((i * tm) // seq, 0, 5)))
    y_s = _final_call(x1_s, ya[n_p:], yb[n_p:], rt_s, mod_s,
                      lambda tm: pl.BlockSpec((tm, d), lambda i: (i, 5)))

    def leaves(lat, kr, kfp, vf, lf, bb, tt):
        return (lat.reshape(1, bb, tt, MLA_KV_RANK),
                kr[:, :MLA_ROPE].reshape(1, bb, tt, MLA_ROPE),
                kfp.reshape(-1, FOX_KV_HEADS, LANES)[:, :, :FOX_HEAD_DIM].reshape(1, bb, tt, FOX_KV_HEADS, FOX_HEAD_DIM),
                vf.reshape(1, bb, tt, FOX_KV_HEADS, FOX_HEAD_DIM),
                lf[:, LF0:LF0 + FOX_HEADS].reshape(1, bb, tt, FOX_HEADS))

    return ((y_p.reshape(bsz, seq, d), y_s.reshape(dbsz, dseq, d))
            + leaves(lat_p, kr_p, kfp_p, vf_p, lf_p, bsz, seq)
            + leaves(lat_s, kr_s, kfp_s, vf_s, lf_s, dbsz, dseq))
```

```python
import functools

import numpy as np
import jax
import jax.numpy as jnp
from jax import lax
from jax.experimental import pallas as pl
from jax.experimental.pallas import tpu as pltpu

F32 = jnp.float32
BF16 = jnp.bfloat16

MLA_HEADS, MLA_NOPE, MLA_ROPE, MLA_V = 8, 64, 32, 64
MLA_Q_RANK, MLA_KV_RANK = 512, 256
FOX_HEADS, FOX_KV_HEADS, FOX_HEAD_DIM = 8, 4, 64
N_GROUPS, EXPERTS_PER_GROUP, TOP_K = 4, 8, 2
N_EXPERTS = N_GROUPS * EXPERTS_PER_GROUP
N_MOD = 6
ROPE_THETA = 10000.0
RMS_EPS = 1e-6
PAGE = 128
LANES = 128
AUG0 = 64
LF0 = 64
DV_AUG = 80

VMEM_LIMIT = 56 * 1024 * 1024
TM_PROJ = 256
TM_GATES = 512
TN_GATES = 1024
TN_MOD = 1024
TQ_ATTN = 512
PAGES_PER_CHUNK = 32
MLA_SUB_KEYS = 1024
PREPASS_PAGES = 256
TG_MOE = 256
TM_FINAL = 512


def _cparams(sem):
    return pltpu.CompilerParams(dimension_semantics=sem, vmem_limit_bytes=VMEM_LIMIT)


def _vmem_whole():
    return pl.BlockSpec(memory_space=pltpu.VMEM)


def _dot(a, b):
    return jnp.dot(a, b, preferred_element_type=F32)


def _dot_nt(a, b):
    return lax.dot_general(a, b, (((1,), (1,)), ((), ())), preferred_element_type=F32)


def _split3(x):
    hi = x.astype(BF16)
    r1 = x - hi.astype(F32)
    mid = r1.astype(BF16)
    lo = (r1 - mid.astype(F32)).astype(BF16)
    return hi, mid, lo


def _group_rsqrt(v, e_ref, inv_cnt):
    ss = _dot((v * v).astype(BF16), e_ref[...])
    return lax.rsqrt(ss * inv_cnt + RMS_EPS)


def _group_bcast(r, et_ref):
    rh = r.astype(BF16)
    rl = (r - rh.astype(F32)).astype(BF16)
    return _dot(rh, et_ref[...]) + _dot(rl, et_ref[...])


def _mod_kernel(c_ref, w_ref, b_ref, o_ref):
    c = c_ref[...]
    a = (c / (1.0 + jnp.exp(-c))).astype(BF16)
    o_ref[...] = _dot(a, w_ref[...].astype(BF16)) + b_ref[...]


def _mod_call(c_all, w_ada, b_ada):
    rows, d = c_all.shape
    n = w_ada.shape[1]
    return pl.pallas_call(
        _mod_kernel,
        out_shape=jax.ShapeDtypeStruct((rows, n), F32),
        grid=(n // TN_MOD,),
        in_specs=[pl.BlockSpec((rows, d), lambda j: (0, 0)),
                  pl.BlockSpec((d, TN_MOD), lambda j: (0, j)),
                  pl.BlockSpec((1, TN_MOD), lambda j: (0, j))],
        out_specs=pl.BlockSpec((rows, TN_MOD), lambda j: (0, j)),
        compiler_params=_cparams(("arbitrary",)),
        name="mod",
    )(c_all, w_ada, b_ada)


C_QA = (0, 512)
C_KVL = (512, 768)
C_FQ = (768, 1792)
C_FK = (1792, 2304)
C_FV = (2304, 2560)
C_MISC = (2560, 2688)
N_SMALL = 2688


def _proj_kernel(x_ref, m0_ref, m1_ref, gattn_ref, wsm_ref, wqb_ref, wkv_ref,
                 gqa_ref, gkva_ref, gq_ref, gkr_ref, gfq_ref, gfk_ref, gkn_ref, bf_ref,
                 tab_ref, eq_ref, eqt_ref, cntq_ref, ef_ref, eft_ref, ek_ref, ekt_ref,
                 pkr_ref, pq_ref, pk_ref, onesq_ref, onesk_ref, wuvt_ref, onesm_ref, wfvt_ref, onesf_ref,
                 tri_ref,
                 hb_ref, qmla_ref, lat_ref, kr_ref, qfa_ref, kfp_ref, kfa_ref, vf_ref,
                 lf_ref, cum_ref, kmla_ref, vmt_ref, vft_ref, carry_ref, *, tiles_per_seq, with_kv):
    i = pl.program_id(0)
    x = x_ref[...]
    xn = x * lax.rsqrt(jnp.mean(x * x, axis=-1, keepdims=True) + RMS_EPS) * gattn_ref[...]
    h = xn * (1.0 + m1_ref[...]) + m0_ref[...]
    hb = h.astype(BF16)
    hb_ref[...] = hb
    z = _dot(hb, wsm_ref[...])
    qa = z[:, C_QA[0]:C_QA[1]]
    kvl = z[:, C_KVL[0]:C_KVL[1]]
    fq = z[:, C_FQ[0]:C_FQ[1]]
    fk = z[:, C_FK[0]:C_FK[1]]
    fv = z[:, C_FV[0]:C_FV[1]]
    misc = z[:, C_MISC[0]:C_MISC[1]]
    cosq, sinq, cosk, sink = tab_ref[0], tab_ref[1], tab_ref[2], tab_ref[3]

    qan = qa * lax.rsqrt(jnp.mean(qa * qa, axis=-1, keepdims=True) + RMS_EPS) * gqa_ref[...]
    qq = _dot(qan.astype(BF16), wqb_ref[...])
    nq = MLA_HEADS * LANES
    q, qsw = qq[:, :nq], qq[:, nq:]
    sc = _group_bcast(_group_rsqrt(q, eq_ref, cntq_ref[...]), eqt_ref)
    gq = gq_ref[...]
    for hh in range(MLA_HEADS):
        sl = slice(hh * LANES, (hh + 1) * LANES)
        qo = sc[:, sl] * (q[:, sl] * (gq[0:1, sl] * cosq) + qsw[:, sl] * (gq[1:2, sl] * sinq))
        qmla_ref[:, sl] = qo.astype(BF16)

    lat = kvl * lax.rsqrt(jnp.mean(kvl * kvl, axis=-1, keepdims=True) + RMS_EPS) * gkva_ref[...]
    lat_ref[...] = lat

    lane = lax.broadcasted_iota(jnp.int32, misc.shape, 1)
    kr2 = jnp.where(lane < MLA_ROPE, misc * misc, 0.0)
    rk = lax.rsqrt(jnp.sum(kr2, axis=-1, keepdims=True) * (1.0 / MLA_ROPE) + RMS_EPS)
    msw = pltpu.roll(misc, LANES - MLA_ROPE, 1)
    gkr = gkr_ref[...]
    kr = rk * (misc * (gkr[0:1] * cosk) + msw * (gkr[1:2] * sink))
    kr_ref[...] = kr

    xl = misc + bf_ref[...]
    lf = jnp.minimum(xl, 0.0) - jnp.log1p(jnp.exp(-jnp.abs(xl)))
    lf = jnp.where((lane >= LF0) & (lane < LF0 + FOX_HEADS), lf, 0.0)
    lf_ref[...] = lf
    l_hi, l_mid, l_lo = _split3(lf)
    tri = tri_ref[...]
    cum = _dot(tri, l_hi) + _dot(tri, l_mid) + _dot(tri, l_lo)
    if tiles_per_seq > 1:
        @pl.when(i % tiles_per_seq == 0)
        def _():
            carry_ref[...] = jnp.zeros_like(carry_ref)
        cum = cum + carry_ref[...]
        carry_ref[...] = cum[cum.shape[0] - 1:, :]
    cum_ref[...] = cum
    c_hi, c_mid, c_lo = _split3(cum)
    cs = jnp.concatenate([c_hi, c_mid, c_lo], axis=1)
    augq = _dot(cs, pq_ref[...]) + onesq_ref[...]
    augk = _dot(cs, pk_ref[...]) + onesk_ref[...]

    scf = _group_bcast(_group_rsqrt(fq, ef_ref, 1.0 / FOX_HEAD_DIM), eft_ref)
    qfa_ref[...] = (fq * scf * gfq_ref[...] + augq).astype(BF16)
    sck = _group_bcast(_group_rsqrt(fk, ek_ref, 1.0 / FOX_HEAD_DIM), ekt_ref)
    kf = fk * sck * gfk_ref[...]
    kfp_ref[...] = kf
    kfa_ref[...] = (kf + augk).astype(BF16)
    vf_ref[...] = fv

    if with_kv:
        lat_b = lat.astype(BF16)
        kn = _dot(lat_b, wkv_ref[...])
        scn = _group_bcast(_group_rsqrt(kn, ef_ref, 1.0 / MLA_NOPE), eft_ref)
        kp = _dot(kr.astype(BF16), pkr_ref[...])
        kmla_ref[...] = (kn * scn * gkn_ref[...] + kp).astype(BF16)
        vmt_ref[...] = (_dot_nt(wuvt_ref[...], lat_b) + onesm_ref[...]).astype(BF16)
        vft_ref[...] = (_dot_nt(wfvt_ref[...], hb) + onesf_ref[...]).astype(BF16)
    else:
        kmla_ref[...] = jnp.zeros_like(kmla_ref)
        vmt_ref[...] = jnp.zeros_like(vmt_ref)
        vft_ref[...] = jnp.zeros_like(vft_ref)


def _proj_call(x2d, m0, m1, mod_specs, tabs, tri, consts, *, seq_rows, with_kv):
    n, d = x2d.shape
    tm = min(TM_PROJ, n)
    tiles_per_seq = max(seq_rows // tm, 1)
    n_tab_tiles = tabs.shape[1] // tm
    row = lambda c: pl.BlockSpec((tm, c), lambda i: (i, 0))
    in_specs = [row(d), mod_specs[0], mod_specs[1]] + [_vmem_whole()] * 12
    in_specs += [pl.BlockSpec((4, tm, LANES), lambda i: (0, i % n_tab_tiles, 0))]
    in_specs += [_vmem_whole()] * 16
    in_specs += [pl.BlockSpec((tm, tm), lambda i: (0, 0))]
    nq = MLA_HEADS * LANES
    out_cols = [(d, BF16), (nq, BF16), (MLA_KV_RANK, F32), (LANES, F32), (FOX_HEADS * LANES, BF16),
                (FOX_KV_HEADS * LANES, F32), (FOX_KV_HEADS * LANES, BF16),
                (FOX_KV_HEADS * FOX_HEAD_DIM, F32),
                (LANES, F32), (LANES, F32), (nq, BF16)]
    out_rows_t = [MLA_HEADS * DV_AUG, FOX_KV_HEADS * DV_AUG]
    col = lambda r: pl.BlockSpec((r, tm), lambda i: (0, i))
    return pl.pallas_call(
        functools.partial(_proj_kernel, tiles_per_seq=tiles_per_seq, with_kv=with_kv),
        out_shape=([jax.ShapeDtypeStruct((n, c), dt) for c, dt in out_cols]
                   + [jax.ShapeDtypeStruct((r, n), BF16) for r in out_rows_t]),
        grid=(n // tm,),
        in_specs=in_specs,
        out_specs=[row(c) for c, _ in out_cols] + [col(r) for r in out_rows_t],
        scratch_shapes=[pltpu.VMEM((1, LANES), F32)],
        compiler_params=_cparams(("arbitrary",)),
        name="proj",
    )(x2d, m0, m1, *consts["proj_a"], tabs, *consts["proj_b"], tri)


def _gates_kernel(h_ref, w_ref, o_ref):
    z = _dot(h_ref[...], w_ref[...])
    o_ref[...] = (1.0 / (1.0 + jnp.exp(-z))).astype(BF16)


def _gates_call(hb, wg):
    n, d = hb.shape
    nc = wg.shape[1]
    tm = min(TM_GATES, n)
    return pl.pallas_call(
        _gates_kernel,
        out_shape=jax.ShapeDtypeStruct((n, nc), BF16),
        grid=(nc // TN_GATES, n // tm),
        in_specs=[pl.BlockSpec((tm, d), lambda j, i: (i, 0)),
                  pl.BlockSpec((d, TN_GATES), lambda j, i: (0, j))],
        out_specs=pl.BlockSpec((tm, TN_GATES), lambda j, i: (i, j)),
        compiler_params=_cparams(("arbitrary", "arbitrary")),
        name="gates",
    )(hb, wg)


def _attn_kernel(q_ref, k_ref, vt_ref, o_ref, m_ref, acc_ref, *, nkv, rep, dv):
    i = pl.program_id(1)
    j = pl.program_id(2)
    tq = q_ref.shape[1]

    @pl.when(j == 0)
    def _():
        m_ref[...] = jnp.full_like(m_ref, -jnp.inf)
        acc_ref[...] = jnp.zeros_like(acc_ref)

    def update(masked):
        for g in range(nkv):
            k = k_ref[0, :, g * LANES:(g + 1) * LANES]
            qs = [q_ref[0, :, (g * rep + e) * LANES:(g * rep + e + 1) * LANES] for e in range(rep)]
            q = qs[0] if rep == 1 else jnp.concatenate(qs, axis=0)
            s = _dot_nt(k, q)
            if masked:
                rows = lax.broadcasted_iota(jnp.int32, s.shape, 0)
                cols = lax.broadcasted_iota(jnp.int32, s.shape, 1)
                s = jnp.where(rows <= cols % tq, s, -jnp.inf)
            m_prev = m_ref[g]
            m_new = jnp.maximum(m_prev, jnp.max(s, axis=0, keepdims=True))
            alpha = jnp.exp(m_prev - m_new)
            p = jnp.exp(s - m_new).astype(BF16)
            acc_ref[g] = alpha * acc_ref[g] + _dot(vt_ref[g * DV_AUG:(g + 1) * DV_AUG, :], p)
            m_ref[g] = m_new

    @pl.when(j < i)
    def _():
        update(False)

    @pl.when(j == i)
    def _():
        update(True)
        for g in range(nkv):
            for e in range(rep):
                a = acc_ref[g][:, e * tq:(e + 1) * tq]
                h = g * rep + e
                o_ref[h * dv:(h + 1) * dv, :] = (a[0:dv] / a[dv:dv + 1]).astype(BF16)


def _attn_call(q, k, vt, *, nkv, rep, dv, name):
    b, t, _ = q.shape
    tq = min(TQ_ATTN, t)
    nq = t // tq
    nh = nkv * rep
    return pl.pallas_call(
        functools.partial(_attn_kernel, nkv=nkv, rep=rep, dv=dv),
        out_shape=jax.ShapeDtypeStruct((nh * dv, b * t), BF16),
        grid=(b, nq, nq),
        in_specs=[pl.BlockSpec((1, tq, q.shape[2]), lambda bb, i, j: (bb, i, 0)),
                  pl.BlockSpec((1, tq, k.shape[2]), lambda bb, i, j: (bb, jnp.minimum(i, j), 0)),
                  pl.BlockSpec((nkv * DV_AUG, tq), lambda bb, i, j: (0, bb * nq + jnp.minimum(i, j)))],
        out_specs=pl.BlockSpec((nh * dv, tq), lambda bb, i, j: (0, bb * nq + i)),
        scratch_shapes=[pltpu.VMEM((nkv, 1, rep * tq), F32), pltpu.VMEM((nkv, DV_AUG, rep * tq), F32)],
        compiler_params=_cparams(("arbitrary", "arbitrary", "arbitrary")),
        name=name,
    )(q, k, vt)


def _softmax_update(s, pv, m_ref, l_ref, acc_ref):
    m_prev = m_ref[...]
    m_new = jnp.maximum(m_prev, jnp.max(s, axis=-1, keepdims=True))
    alpha = jnp.exp(m_prev - m_new)
    p = jnp.exp(s - m_new)
    l_ref[...] = alpha * l_ref[...] + jnp.sum(p, axis=-1, keepdims=True)
    acc_ref[...] = alpha * acc_ref[...] + pv(p.astype(BF16))
    m_ref[...] = m_new


def _chunk_loop(b, n_seq, n_chunks, start, wait, compute):
    @pl.when(b == 0)
    def _():
        start(b, 0, 0)

    def pair(c2, carry):
        c = 2 * c2
        start(b, c + 1, 1)
        wait(0)
        compute(0)

        @pl.when(c + 2 < n_chunks)
        def _():
            start(b, c + 2, 0)

        @pl.when(jnp.logical_and(c + 2 >= n_chunks, b + 1 < n_seq))
        def _():
            start(b + 1, 0, 0)

        wait(1)
        compute(1)
        return carry

    lax.fori_loop(0, n_chunks // 2, pair, 0)


def _head_select(full, dv):
    rows = lax.broadcasted_iota(jnp.int32, full.shape, 0)
    cols = lax.broadcasted_iota(jnp.int32, full.shape, 1)
    nh = full.shape[1] // dv
    sel = jnp.where(cols // dv == rows % nh, full, 0.0)
    return jnp.sum(sel.reshape(full.shape[0] // nh, nh, full.shape[1]), axis=1)


def _paged_mla_kernel(pt_ref, qn_ref, qr_ref, latn_ref, krn_ref, wukt_ref, gcol_ref, wuv_ref,
                      clat_ref, ckr_ref, o_ref,
                      a_ref, wg_ref, latbuf, krbuf, newlat, newkr, sem, m_ref, l_ref, acc_ref,
                      *, n_seq, n_chunks, pages):
    b = pl.program_id(0)
    ck = pages * PAGE
    sub = min(MLA_SUB_KEYS, ck)
    nrow = qn_ref.shape[1]
    nup = wukt_ref.shape[0]
    t_new = latn_ref.shape[1]

    def copies(seq, k, slot):
        out = []
        for p in range(pages):
            pid = pt_ref[seq, k * pages + p]
            out.append(pltpu.make_async_copy(clat_ref.at[pid], latbuf.at[slot, pl.ds(p * PAGE, PAGE)],
                                             sem.at[0, slot]))
            out.append(pltpu.make_async_copy(ckr_ref.at[pid], krbuf.at[slot, :, pl.ds(p * PAGE, PAGE)],
                                             sem.at[1, slot]))
        return out

    def start(seq, k, slot):
        for c in copies(seq, k, slot):
            c.start()

    def wait(slot):
        for c in copies(0, 0, slot):
            c.wait()

    @pl.when(b == 0)
    def _():
        w = wukt_ref[...]
        a_ref[0:nup, :] = w.astype(BF16)
        wg_ref[...] = (w * gcol_ref[...]).astype(BF16)
        newlat[...] = jnp.zeros_like(newlat)
        newkr[...] = jnp.zeros_like(newkr)

    a_ref[nup:nup + nrow, :] = _dot(qn_ref[0], wg_ref[...]).astype(BF16)
    m_ref[...] = jnp.full_like(m_ref, -jnp.inf)
    l_ref[...] = jnp.zeros_like(l_ref)
    acc_ref[...] = jnp.zeros_like(acc_ref)
    qr = qr_ref[0]

    def nope_scores(lat_b):
        r_all = _dot_nt(a_ref[...], lat_b)
        kup = r_all[0:nup]
        nk = kup.shape[1]
        ss = jnp.sum((kup * kup).reshape(MLA_HEADS, MLA_NOPE, nk), axis=1)
        rn = lax.rsqrt(ss * (1.0 / MLA_NOPE) + RMS_EPS)
        sn = r_all[nup:nup + nrow].reshape(nrow // MLA_HEADS, MLA_HEADS, nk) * rn[None]
        return sn.reshape(nrow, nk)

    def compute(slot):
        lat_b = latbuf[slot].astype(BF16)
        parts = [nope_scores(lat_b[j * sub:(j + 1) * sub]) for j in range(ck // sub)]
        s = jnp.concatenate(parts, axis=1) + _dot(qr, krbuf[slot].astype(BF16))
        _softmax_update(s, lambda p: _dot(p, lat_b), m_ref, l_ref, acc_ref)

    _chunk_loop(b, n_seq, n_chunks, start, wait, compute)

    newlat[0:t_new, :] = latn_ref[0]
    newkr[0:t_new, :] = krn_ref[0]
    lat_n = newlat[...].astype(BF16)
    s = nope_scores(lat_n) + _dot_nt(qr, newkr[...].astype(BF16))
    rows = lax.broadcasted_iota(jnp.int32, s.shape, 0)
    cols = lax.broadcasted_iota(jnp.int32, s.shape, 1)
    s = jnp.where(cols <= rows // MLA_HEADS, s, -jnp.inf)
    _softmax_update(s, lambda p: _dot(p, lat_n), m_ref, l_ref, acc_ref)

    o_lat = (acc_ref[...] / l_ref[...]).astype(BF16)
    o_ref[0] = _head_select(_dot(o_lat, wuv_ref[...]), MLA_V).astype(BF16)


def _paged_mla_call(pt, qn_bd, qr, lat_new, kr_new, wukt, gcol, wuv, clat, ckr):
    n_seq, n_pages = pt.shape
    pages = min(PAGES_PER_CHUNK, n_pages // 2)
    n_chunks = n_pages // pages
    ck = pages * PAGE
    nrow = qn_bd.shape[1]
    t_new = lat_new.shape[1]
    nup = wukt.shape[0]
    seq3 = lambda a: pl.BlockSpec((1,) + a.shape[1:], lambda b, pt_: (b, 0, 0))
    anyspec = pl.BlockSpec(memory_space=pl.ANY)
    grid_spec = pltpu.PrefetchScalarGridSpec(
        num_scalar_prefetch=1, grid=(n_seq,),
        in_specs=[seq3(qn_bd), seq3(qr), seq3(lat_new), seq3(kr_new),
                  _vmem_whole(), _vmem_whole(), _vmem_whole(), anyspec, anyspec],
        out_specs=pl.BlockSpec((1, t_new, MLA_HEADS * MLA_V), lambda b, pt_: (b, 0, 0)),
        scratch_shapes=[pltpu.VMEM((nup + nrow, MLA_KV_RANK), BF16),
                        pltpu.VMEM((nup, MLA_KV_RANK), BF16),
                        pltpu.VMEM((2, ck, MLA_KV_RANK), F32),
                        pltpu.VMEM((2, MLA_ROPE, ck), F32),
                        pltpu.VMEM((LANES, MLA_KV_RANK), F32),
                        pltpu.VMEM((LANES, MLA_ROPE), F32),
                        pltpu.SemaphoreType.DMA((2, 2)),
                        pltpu.VMEM((nrow, 1), F32), pltpu.VMEM((nrow, 1), F32),
                        pltpu.VMEM((nrow, MLA_KV_RANK), F32)])
    return pl.pallas_call(
        functools.partial(_paged_mla_kernel, n_seq=n_seq, n_chunks=n_chunks, pages=pages),
        out_shape=jax.ShapeDtypeStruct((n_seq, t_new, MLA_HEADS * MLA_V), BF16),
        grid_spec=grid_spec,
        compiler_params=_cparams(("arbitrary",)),
        name="paged_mla",
    )(pt, qn_bd, qr, lat_new, kr_new, wukt, gcol, wuv, clat, ckr)


def _suffix_kernel(x_ref, triu_ref, ones_ref, o_ref):
    npg, nh, pg = x_ref.shape
    x = x_ref[...].reshape(npg * nh, pg)
    e = jnp.zeros(x.shape, F32)
    t = jnp.zeros(x.shape, F32)
    for piece in _split3(x):
        e = e + _dot(piece, triu_ref[...])
        t = t + _dot(piece, ones_ref[...])
    o_ref[:, 0:nh, :] = e.reshape(npg, nh, pg)
    o_ref[:, nh:2 * nh, :] = t.reshape(npg, nh, pg)


def _suffix_call(lft, triu, ones):
    n_pool, nh, pg = lft.shape
    npg = min(PREPASS_PAGES, n_pool)
    assert n_pool % npg == 0
    return pl.pallas_call(
        _suffix_kernel,
        out_shape=jax.ShapeDtypeStruct((n_pool, 2 * nh, pg), F32),
        grid=(n_pool // npg,),
        in_specs=[pl.BlockSpec((npg, nh, pg), lambda i: (i, 0, 0)), _vmem_whole(), _vmem_whole()],
        out_specs=pl.BlockSpec((npg, 2 * nh, pg), lambda i: (i, 0, 0)),
        compiler_params=_cparams(("arbitrary",)),
        name="suffix",
    )(lft, triu, ones)


def _paged_fox_kernel(pt_ref, qbd_ref, kn_ref, vn_ref, cncol_ref, cnt_ref, psel_ref,
                      ck_ref, cv_ref, cet_ref, o_ref,
                      kbuf, vbuf, etbuf, newk, newv, sem, m_ref, l_ref, acc_ref, sufc_ref,
                      *, n_seq, n_chunks, pages):
    b = pl.program_id(0)
    ck = pages * PAGE
    nrow = qbd_ref.shape[1]
    t_new = kn_ref.shape[1]
    nt = nrow // FOX_HEADS

    def copies(seq, k, slot):
        out = []
        kk = n_chunks - 1 - k
        for p in range(pages):
            pid = pt_ref[seq, kk * pages + p]
            dst = pl.ds(p * PAGE, PAGE)
            out.append(pltpu.make_async_copy(ck_ref.at[pid], kbuf.at[slot, :, dst], sem.at[0, slot]))
            out.append(pltpu.make_async_copy(cv_ref.at[pid], vbuf.at[slot, :, dst], sem.at[1, slot]))
            out.append(pltpu.make_async_copy(cet_ref.at[pid], etbuf.at[slot, :, dst], sem.at[2, slot]))
        return out

    def start(seq, k, slot):
        for c in copies(seq, k, slot):
            c.start()

    def wait(slot):
        for c in copies(0, 0, slot):
            c.wait()

    @pl.when(b == 0)
    def _():
        newk[...] = jnp.zeros_like(newk)
        newv[...] = jnp.zeros_like(newv)

    m_ref[...] = jnp.full_like(m_ref, -jnp.inf)
    l_ref[...] = jnp.zeros_like(l_ref)
    acc_ref[...] = jnp.zeros_like(acc_ref)
    sufc_ref[...] = jnp.zeros_like(sufc_ref)
    qbd = qbd_ref[0]
    cncol = cncol_ref[0]

    def compute(slot):
        s = _dot(qbd, kbuf[slot].astype(BF16))
        carry = sufc_ref[...]
        pieces = [None] * pages
        for p in reversed(range(pages)):
            lanes_p = slice(p * PAGE, (p + 1) * PAGE)
            pieces[p] = etbuf[slot, 0:FOX_HEADS, lanes_p] + carry
            carry = carry + etbuf[slot, FOX_HEADS:2 * FOX_HEADS, lanes_p]
        sufc_ref[...] = carry
        bias = jnp.concatenate(pieces, axis=1)
        s = (s.reshape(nt, FOX_HEADS, ck) + bias[None]).reshape(nrow, ck) + cncol
        v_b = vbuf[slot].astype(BF16)
        _softmax_update(s, lambda p: _dot_nt(p, v_b), m_ref, l_ref, acc_ref)

    _chunk_loop(b, n_seq, n_chunks, start, wait, compute)

    newk[0:t_new, :] = kn_ref[0]
    newv[0:t_new, :] = vn_ref[0]
    s = _dot_nt(qbd, newk[...].astype(BF16))
    s = (s.reshape(nt, FOX_HEADS, LANES) - cnt_ref[0][None]).reshape(nrow, LANES) + cncol
    rows = lax.broadcasted_iota(jnp.int32, s.shape, 0)
    cols = lax.broadcasted_iota(jnp.int32, s.shape, 1)
    s = jnp.where(cols <= rows // FOX_HEADS, s, -jnp.inf)
    v_n = newv[...].astype(BF16)
    _softmax_update(s, lambda p: _dot(p, v_n), m_ref, l_ref, acc_ref)

    o = (acc_ref[...] / l_ref[...]).astype(BF16)
    o_ref[0] = _head_select(_dot(o, psel_ref[...]), FOX_HEAD_DIM).astype(BF16)


def _paged_fox_call(pt, qbd, k_new, v_new, cn_col, cn_t, psel, cck, ccv, clft):
    n_seq, n_pages = pt.shape
    pages = min(PAGES_PER_CHUNK, n_pages // 2)
    n_chunks = n_pages // pages
    ck = pages * PAGE
    nrow = qbd.shape[1]
    t_new = k_new.shape[1]
    kvw = FOX_KV_HEADS * FOX_HEAD_DIM
    seq3 = lambda a: pl.BlockSpec((1,) + a.shape[1:], lambda b, pt_: (b, 0, 0))
    anyspec = pl.BlockSpec(memory_space=pl.ANY)
    grid_spec = pltpu.PrefetchScalarGridSpec(
        num_scalar_prefetch=1, grid=(n_seq,),
        in_specs=[seq3(qbd), seq3(k_new), seq3(v_new), seq3(cn_col), seq3(cn_t), _vmem_whole(),
                  anyspec, anyspec, anyspec],
        out_specs=pl.BlockSpec((1, t_new, FOX_HEADS * FOX_HEAD_DIM), lambda b, pt_: (b, 0, 0)),
        scratch_shapes=[pltpu.VMEM((2, kvw, ck), F32), pltpu.VMEM((2, kvw, ck), F32),
                        pltpu.VMEM((2, 2 * FOX_HEADS, ck), F32),
                        pltpu.VMEM((LANES, kvw), F32), pltpu.VMEM((LANES, kvw), F32),
                        pltpu.SemaphoreType.DMA((3, 2)),
                        pltpu.VMEM((nrow, 1), F32), pltpu.VMEM((nrow, 1), F32),
                        pltpu.VMEM((nrow, kvw), F32), pltpu.VMEM((FOX_HEADS, LANES), F32)])
    return pl.pallas_call(
        functools.partial(_paged_fox_kernel, n_seq=n_seq, n_chunks=n_chunks, pages=pages),
        out_shape=jax.ShapeDtypeStruct((n_seq, t_new, FOX_HEADS * FOX_HEAD_DIM), BF16),
        grid_spec=grid_spec,
        compiler_params=_cparams(("arbitrary",)),
        name="paged_fox",
    )(pt, qbd, k_new, v_new, cn_col, cn_t, psel, cck, ccv, clft)


def _merge_kernel(x_ref, om_ref, of_ref, sg_ref, m2_ref, m3_ref, m4_ref, gffn_ref,
                  wba_ref, wbb_ref, wo_ref, wrh_ref, wrl_ref, br_ref,
                  x1_ref, h2_ref, rt_ref, *, o_transposed):
    d = x_ref.shape[1]
    if o_transposed:
        tn = lambda o, w: lax.dot_general(o, w, (((0,), (0,)), ((), ())), preferred_element_type=F32)
        a = tn(om_ref[...], wba_ref[...])
        bb = tn(of_ref[...], wbb_ref[...])
    else:
        a = _dot(om_ref[...], wba_ref[...])
        bb = _dot(of_ref[...], wbb_ref[...])
    u = sg_ref[:, 0:d].astype(F32) * a + sg_ref[:, d:2 * d].astype(F32) * bb
    x1 = x_ref[...] + m2_ref[...] * _dot(u.astype(BF16), wo_ref[...])
    x1_ref[...] = x1
    xn = x1 * lax.rsqrt(jnp.mean(x1 * x1, axis=-1, keepdims=True) + RMS_EPS) * gffn_ref[...]
    h2 = xn * (1.0 + m4_ref[...]) + m3_ref[...]
    h2_hi = h2.astype(BF16)
    h2_ref[...] = h2_hi
    h2_lo = (h2 - h2_hi.astype(F32)).astype(BF16)
    lg = (_dot(h2_hi, wrh_ref[...]) + _dot(h2_hi, wrl_ref[...]) + _dot(h2_lo, wrh_ref[...])
          + br_ref[...])
    lane = lax.broadcasted_iota(jnp.int32, lg.shape, 1)
    neg = -jnp.inf
    is_g = (lane >= N_EXPERTS) & (lane < N_EXPERTS + N_GROUPS)
    lgm = jnp.where(is_g, lg, neg)
    mg = jnp.max(lgm, axis=-1, keepdims=True)
    eg = jnp.exp(lgm - mg)
    pgrp = eg / jnp.sum(eg, axis=-1, keepdims=True)
    pg_top = jnp.max(pgrp, axis=-1, keepdims=True)
    g_idx = jnp.min(jnp.where(is_g & (pgrp == pg_top), lane - N_EXPERTS, N_GROUPS), axis=-1, keepdims=True)
    sel = (lane < N_EXPERTS) & (lane // EXPERTS_PER_GROUP == g_idx)
    em = jnp.where(sel, lg, neg)
    me = jnp.max(em, axis=-1, keepdims=True)
    ee = jnp.exp(em - me)
    pe = ee / jnp.sum(ee, axis=-1, keepdims=True)
    big = 4 * LANES
    p1 = jnp.max(jnp.where(sel, pe, -1.0), axis=-1, keepdims=True)
    i1 = jnp.min(jnp.where(sel & (pe == p1), lane, big), axis=-1, keepdims=True)
    sel2 = sel & (lane != i1)
    p2 = jnp.max(jnp.where(sel2, pe, -1.0), axis=-1, keepdims=True)
    i2 = jnp.min(jnp.where(sel2 & (pe == p2), lane, big), axis=-1, keepdims=True)
    den = p1 + p2
    w1 = pg_top * p1 / den
    w2 = pg_top * p2 / den
    rt = jnp.where(lane == 0, i1.astype(F32),
                   jnp.where(lane == 1, i2.astype(F32),
                             jnp.where(lane == 2, w1, jnp.where(lane == 3, w2, 0.0))))
    rt_ref[...] = rt


def _merge_call(x2d, o_mla, o_fox, sg, m2, m3, m4, mod_specs, consts, *, o_transposed):
    n, d = x2d.shape
    tm = min(TM_PROJ, n)
    row = lambda c: pl.BlockSpec((tm, c), lambda i: (i, 0))
    col = lambda r: pl.BlockSpec((r, tm), lambda i: (0, i))
    o_spec = (lambda o: col(o.shape[0])) if o_transposed else (lambda o: row(o.shape[1]))
    in_specs = [row(d), o_spec(o_mla), o_spec(o_fox), row(sg.shape[1]),
                mod_specs[0], mod_specs[1], mod_specs[2]] + [_vmem_whole()] * 7
    return pl.pallas_call(
        functools.partial(_merge_kernel, o_transposed=o_transposed),
        out_shape=[jax.ShapeDtypeStruct((n, d), F32), jax.ShapeDtypeStruct((n, d), BF16),
                   jax.ShapeDtypeStruct((n, LANES), F32)],
        grid=(n // tm,),
        in_specs=in_specs,
        out_specs=[row(d), row(d), row(LANES)],
        compiler_params=_cparams(("arbitrary",)),
        name="merge",
    )(x2d, o_mla, o_fox, sg, m2, m3, m4, *consts["merge"])


def _gmm_kernel(te_ref, nv_ref, x_ref, wg_ref, wu_ref, wd_ref, o_ref, wgb, wub, wdb):
    i = pl.program_id(0)
    valid = i < nv_ref[0]
    prev = te_ref[jnp.maximum(i - 1, 0)]
    changed = jnp.logical_or(i == 0, te_ref[i] != prev)

    @pl.when(jnp.logical_and(valid, changed))
    def _():
        wgb[...] = wg_ref[0].astype(BF16)
        wub[...] = wu_ref[0].astype(BF16)
        wdb[...] = wd_ref[0].astype(BF16)

    @pl.when(valid)
    def _():
        x = x_ref[...]
        g = _dot(x, wgb[...])
        u = _dot(x, wub[...])
        hid = (g / (1.0 + jnp.exp(-g))) * u
        o_ref[...] = _dot(hid.astype(BF16), wdb[...]).astype(BF16)

    @pl.when(jnp.logical_not(valid))
    def _():
        o_ref[...] = jnp.zeros_like(o_ref)


def _gmm_call(tile_e, n_valid, xs, w_gate, w_up, w_down):
    p_total, d = xs.shape
    de = w_gate.shape[2]
    n_tiles = p_total // TG_MOE
    grid_spec = pltpu.PrefetchScalarGridSpec(
        num_scalar_prefetch=2, grid=(n_tiles,),
        in_specs=[pl.BlockSpec((TG_MOE, d), lambda i, te, nv: (jnp.minimum(i, nv[0] - 1), 0)),
                  pl.BlockSpec((1, d, de), lambda i, te, nv: (te[i], 0, 0)),
                  pl.BlockSpec((1, d, de), lambda i, te, nv: (te[i], 0, 0)),
                  pl.BlockSpec((1, de, d), lambda i, te, nv: (te[i], 0, 0))],
        out_specs=pl.BlockSpec((TG_MOE, d), lambda i, te, nv: (i, 0)),
        scratch_shapes=[pltpu.VMEM((d, de), BF16), pltpu.VMEM((d, de), BF16), pltpu.VMEM((de, d), BF16)])
    return pl.pallas_call(
        _gmm_kernel,
        out_shape=jax.ShapeDtypeStruct((p_total, d), BF16),
        grid_spec=grid_spec,
        compiler_params=_cparams(("arbitrary",)),
        name="gmm",
    )(tile_e, n_valid, xs, w_gate, w_up, w_down)


def _final_kernel(x1_ref, ya_ref, yb_ref, rt_ref, m5_ref, o_ref):
    rt = rt_ref[...]
    ffn = rt[:, 2:3] * ya_ref[...].astype(F32) + rt[:, 3:4] * yb_ref[...].astype(F32)
    o_ref[...] = x1_ref[...] + m5_ref[...] * ffn


def _final_call(x1, ya, yb, rt, m5, m5_spec_fn):
    n, d = x1.shape
    tm = min(TM_FINAL, n)
    row = lambda c: pl.BlockSpec((tm, c), lambda i: (i, 0))
    return pl.pallas_call(
        _final_kernel,
        out_shape=jax.ShapeDtypeStruct((n, d), F32),
        grid=(n // tm,),
        in_specs=[row(d), row(d), row(d), row(LANES), m5_spec_fn(tm)],
        out_specs=row(d),
        compiler_params=_cparams(("arbitrary",)),
        name="final",
    )(x1, ya, yb, rt, m5)


def _np_indicator(n_rows, groups):
    e = np.zeros((n_rows, LANES), np.float32)
    for col, (lo, hi) in enumerate(groups):
        e[lo:hi, col] = 1.0
    return e


def _static_consts():
    c = {}
    nq = MLA_HEADS * LANES
    eq = _np_indicator(nq, [(h * LANES, h * LANES + MLA_NOPE) for h in range(MLA_HEADS)]
                       + [(h * LANES + MLA_NOPE, h * LANES + MLA_NOPE + MLA_ROPE) for h in range(MLA_HEADS)])
    cnt = np.ones((1, LANES), np.float32)
    cnt[0, :MLA_HEADS] = 1.0 / MLA_NOPE
    cnt[0, MLA_HEADS:2 * MLA_HEADS] = 1.0 / MLA_ROPE
    ef = _np_indicator(nq, [(h * LANES, h * LANES + FOX_HEAD_DIM) for h in range(FOX_HEADS)])
    ek = _np_indicator(FOX_KV_HEADS * LANES, [(g * LANES, g * LANES + FOX_HEAD_DIM) for g in range(FOX_KV_HEADS)])
    pkr = np.zeros((LANES, nq), np.float32)
    for h in range(MLA_HEADS):
        for j in range(MLA_ROPE):
            pkr[j, h * LANES + MLA_NOPE + j] = 1.0
    pq = np.zeros((3 * LANES, FOX_HEADS * LANES), np.float32)
    pk = np.zeros((3 * LANES, FOX_KV_HEADS * LANES), np.float32)
    onesq = np.zeros((1, FOX_HEADS * LANES), np.float32)
    onesk = np.zeros((1, FOX_KV_HEADS * LANES), np.float32)
    rep = FOX_HEADS // FOX_KV_HEADS
    for h in range(FOX_HEADS):
        g, e = h // rep, h % rep
        for piece in range(3):
            pq[piece * LANES + LF0 + h, h * LANES + AUG0 + piece] = 1.0
            pk[piece * LANES + LF0 + h, g * LANES + AUG0 + 3 + 3 * e + piece] = -1.0
            onesq[0, h * LANES + AUG0 + 3 + 3 * e + piece] = 1.0
    for g in range(FOX_KV_HEADS):
        onesk[0, g * LANES + AUG0:g * LANES + AUG0 + 3] = 1.0
    psel = np.zeros((FOX_KV_HEADS * FOX_HEAD_DIM, FOX_HEADS * FOX_HEAD_DIM), np.float32)
    for h in range(FOX_HEADS):
        g = h // rep
        for dd in range(FOX_HEAD_DIM):
            psel[g * FOX_HEAD_DIM + dd, h * FOX_HEAD_DIM + dd] = 1.0
    b = lambda a: jnp.asarray(a, BF16)
    kk = np.arange(PAGE)
    c.update(triu=b(kk[:, None] > kk[None, :]), ones=b(np.ones((PAGE, PAGE), np.float32)))
    c.update(eq=b(eq), eqt=b(eq.T), cntq=jnp.asarray(cnt), ef=b(ef), eft=b(ef.T), ek=b(ek), ekt=b(ek.T),
             pkr=b(pkr), pq=b(pq), pk=b(pk), onesq=jnp.asarray(onesq), onesk=jnp.asarray(onesk), psel=b(psel))
    return c


def _pad_heads(w, nh, dh):
    k = w.shape[0]
    return jnp.pad(w.reshape(k, nh, dh), ((0, 0), (0, 0), (0, LANES - dh))).reshape(k, nh * LANES)


def _head_row(g, nh, scale=1.0):
    dh = g.shape[0]
    return jnp.tile(jnp.pad(g * scale, (0, LANES - dh)), nh)[None, :]


def _rope_tables(pos):
    half = MLA_ROPE // 2
    inv = ROPE_THETA ** (-jnp.arange(half, dtype=F32) / half)
    ang = pos.astype(F32)[:, None] * inv[None, :]
    cos, sin = jnp.cos(ang), jnp.sin(ang)
    t = pos.shape[0]
    z = lambda n: jnp.zeros((t, n), F32)
    cosq = jnp.concatenate([jnp.ones((t, MLA_NOPE), F32), cos, cos, z(LANES - MLA_NOPE - MLA_ROPE)], axis=1)
    sinq = jnp.concatenate([z(MLA_NOPE), -sin, sin, z(LANES - MLA_NOPE - MLA_ROPE)], axis=1)
    cosk = jnp.concatenate([cos, cos, z(LANES - MLA_ROPE)], axis=1)
    sink = jnp.concatenate([-sin, sin, z(LANES - MLA_ROPE)], axis=1)
    return jnp.stack([cosq, sinq, cosk, sink], axis=0)


def kernel(x_prompt, x_sample, cache_mla_latent, cache_mla_krope, cache_fox_k, cache_fox_v, cache_fox_logf,
           page_table, c_prompt, c_sample, w_ada, b_ada, g_attn, w_in, g_qa, w_qb, g_qn, g_qr, g_kva, g_kr,
           w_uk, w_uv, g_kn, g_fq, g_fk, b_f, w_br_a, w_br_b, w_o, g_ffn, w_rg, b_rg, w_re, b_re,
           w_e_gate, w_e_up, w_e_down):
    depth = w_ada.shape[0]
    assert depth == 1, "single-layer trunk"
    bsz, seq, d = x_prompt.shape
    dbsz, dseq, _ = x_sample.shape
    n_pool = cache_mla_latent.shape[1]
    past = page_table.shape[1] * PAGE
    n_p, n_s = bsz * seq, dbsz * dseq
    n_tok = n_p + n_s
    sc = _static_consts()
    half = MLA_ROPE // 2
    scale_mla = float((MLA_NOPE + MLA_ROPE) ** -0.5)
    scale_fox = float(FOX_HEAD_DIM ** -0.5)

    n_c = bsz + dbsz
    n_c_pad = -(-n_c // 8) * 8
    c_all = jnp.pad(jnp.concatenate([c_prompt, c_sample], axis=0), ((0, n_c_pad - n_c), (0, 0)))
    mod = _mod_call(c_all, w_ada[0], b_ada[0][None, :])
    mod3 = mod.reshape(n_c_pad, 1, N_MOD * d)
    mod_s = jnp.repeat(mod[bsz:bsz + dbsz], dseq, axis=0)

    def mod_spec_p(k, tm):
        return pl.BlockSpec((None, 1, d), lambda i: ((i * tm) // seq, 0, k))

    def mod_spec_s(k, tm):
        return pl.BlockSpec((tm, d), lambda i: (i, k))

    qa_w, kvl_w, kr_w, fq_w, fk_w, fv_w, fl_w, ga_w, gb_w = jnp.split(
        w_in[0], np.cumsum([MLA_Q_RANK, MLA_KV_RANK, MLA_ROPE, FOX_HEADS * FOX_HEAD_DIM,
                            FOX_KV_HEADS * FOX_HEAD_DIM, FOX_KV_HEADS * FOX_HEAD_DIM, FOX_HEADS, d]).tolist(), axis=1)
    kr_sw = jnp.concatenate([kr_w[:, half:], kr_w[:, :half]], axis=1)
    misc_w = jnp.concatenate([kr_w, kr_sw, fl_w, jnp.zeros((d, LANES - 2 * MLA_ROPE - FOX_HEADS), F32)], axis=1)
    w_small = jnp.concatenate([qa_w, kvl_w, _pad_heads(fq_w, FOX_HEADS, FOX_HEAD_DIM),
                               _pad_heads(fk_w, FOX_KV_HEADS, FOX_HEAD_DIM), fv_w, misc_w], axis=1).astype(BF16)
    w_gates = jnp.concatenate([ga_w, gb_w], axis=1).astype(BF16)
    dq = MLA_NOPE + MLA_ROPE
    wqb3 = w_qb[0].reshape(MLA_Q_RANK, MLA_HEADS, dq)
    wqb_p = jnp.pad(wqb3, ((0, 0), (0, 0), (0, LANES - dq))).reshape(MLA_Q_RANK, MLA_HEADS * LANES)
    wqb_sw3 = jnp.concatenate([jnp.zeros((MLA_Q_RANK, MLA_HEADS, MLA_NOPE), F32),
                               wqb3[:, :, MLA_NOPE + half:], wqb3[:, :, MLA_NOPE:MLA_NOPE + half],
                               jnp.zeros((MLA_Q_RANK, MLA_HEADS, LANES - dq), F32)], axis=2)
    wqb_cat = jnp.concatenate([wqb_p, wqb_sw3.reshape(MLA_Q_RANK, MLA_HEADS * LANES)], axis=1).astype(BF16)
    wuk2 = w_uk[0].reshape(MLA_KV_RANK, MLA_HEADS * MLA_NOPE)
    wuv2 = w_uv[0].reshape(MLA_KV_RANK, MLA_HEADS * MLA_V)
    wkv = _pad_heads(wuk2, MLA_HEADS, MLA_NOPE).astype(BF16)

    def rows_aug(w, nh, dh):
        return jnp.pad(w.T.reshape(nh, dh, w.shape[0]), ((0, 0), (0, DV_AUG - dh), (0, 0))).reshape(nh * DV_AUG, -1)

    def ones_aug(nh, dh):
        o = np.zeros((nh, DV_AUG, 1), np.float32)
        o[:, dh:, :] = 1.0
        return jnp.asarray(o.reshape(nh * DV_AUG, 1))

    wuvt = rows_aug(wuv2, MLA_HEADS, MLA_V).astype(BF16)
    wfvt = rows_aug(fv_w, FOX_KV_HEADS, FOX_HEAD_DIM).astype(BF16)

    gqr = g_qr[0]
    gq_a = jnp.concatenate([g_qn[0], gqr, jnp.zeros((LANES - dq,), F32)])
    gq_b = jnp.concatenate([jnp.zeros((MLA_NOPE,), F32), gqr[half:], gqr[:half], jnp.zeros((LANES - dq,), F32)])
    gq = jnp.stack([jnp.tile(gq_a, MLA_HEADS), jnp.tile(gq_b, MLA_HEADS)], axis=0) * scale_mla
    gkr0 = g_kr[0]
    gkr = jnp.stack([jnp.pad(gkr0, (0, LANES - MLA_ROPE)),
                     jnp.pad(jnp.concatenate([gkr0[half:], gkr0[:half]]), (0, LANES - MLA_ROPE))], axis=0)
    bf_row = jnp.zeros((1, LANES), F32).at[0, LF0:LF0 + FOX_HEADS].set(b_f[0])
    proj_a = [g_attn, w_small, wqb_cat, wkv, g_qa, g_kva, gq, gkr,
              _head_row(g_fq[0], FOX_HEADS, scale_fox), _head_row(g_fk[0], FOX_KV_HEADS),
              _head_row(g_kn[0], MLA_HEADS), bf_row]
    proj_b = [sc["eq"], sc["eqt"], sc["cntq"], sc["ef"], sc["eft"], sc["ek"], sc["ekt"],
              sc["pkr"], sc["pq"], sc["pk"], sc["onesq"], sc["onesk"],
              wuvt, ones_aug(MLA_HEADS, MLA_V), wfvt, ones_aug(FOX_KV_HEADS, FOX_HEAD_DIM)]
    w_r = jnp.concatenate([w_re[0], w_rg[0], jnp.zeros((d, LANES - N_EXPERTS - N_GROUPS), F32)], axis=1)
    w_r_hi = w_r.astype(BF16)
    w_r_lo = (w_r - w_r_hi.astype(F32)).astype(BF16)
    b_r = jnp.concatenate([b_re[0], b_rg[0], jnp.zeros((LANES - N_EXPERTS - N_GROUPS,), F32)])[None, :]
    consts = {"proj_a": proj_a, "proj_b": proj_b,
              "merge": [g_ffn, w_br_a[0].astype(BF16), w_br_b[0].astype(BF16), w_o[0].astype(BF16),
                        w_r_hi, w_r_lo, b_r]}

    tm_p = min(TM_PROJ, n_p)
    tm_s = min(TM_PROJ, n_s)
    tabs_p = _rope_tables(jnp.arange(seq, dtype=jnp.int32))
    tabs_s = _rope_tables(past + (jnp.arange(tm_s, dtype=jnp.int32) % dseq))
    r_p = np.arange(tm_p)
    tri_p = jnp.asarray(r_p[None, :] <= r_p[:, None], BF16)
    r_s = np.arange(tm_s)
    tri_s = jnp.asarray((r_s[None, :] <= r_s[:, None]) & (r_s[None, :] // dseq == r_s[:, None] // dseq), BF16)
    xp2 = x_prompt.reshape(n_p, d)
    xs2 = x_sample.reshape(n_s, d)
    (hb_p, qmla_p, lat_p, kr_p, qfa_p, kfp_p, kfa_p, vf_p, lf_p, cum_p, kmla_p, vmt_p, vft_p) = _proj_call(
        xp2, mod3, mod3, (mod_spec_p(0, tm_p), mod_spec_p(1, tm_p)), tabs_p, tri_p, consts,
        seq_rows=seq, with_kv=True)
    (hb_s, qmla_s, lat_s, kr_s, qfa_s, kfp_s, kfa_s, vf_s, lf_s, cum_s, _, _, _) = _proj_call(
        xs2, mod_s, mod_s, (mod_spec_s(0, tm_s), mod_spec_s(1, tm_s)), tabs_s, tri_s, consts,
        seq_rows=tm_s, with_kv=False)

    sg_p = _gates_call(hb_p, w_gates)
    sg_s = _gates_call(hb_s, w_gates)

    o_mla_p = _attn_call(qmla_p.reshape(bsz, seq, -1), kmla_p.reshape(bsz, seq, -1), vmt_p,
                         nkv=MLA_HEADS, rep=1, dv=MLA_V, name="attn_mla")
    o_fox_p = _attn_call(qfa_p.reshape(bsz, seq, -1), kfa_p.reshape(bsz, seq, -1), vft_p,
                         nkv=FOX_KV_HEADS, rep=FOX_HEADS // FOX_KV_HEADS, dv=FOX_HEAD_DIM, name="attn_fox")

    nrow = dseq * MLA_HEADS
    q4 = qmla_s.reshape(dbsz, dseq, MLA_HEADS, LANES)
    eye_h = jnp.eye(MLA_HEADS, dtype=BF16)
    qn_bd = (q4[:, :, :, None, :MLA_NOPE] * eye_h[None, None, :, :, None]).reshape(dbsz, nrow, MLA_HEADS * MLA_NOPE)
    qr = q4[..., MLA_NOPE:dq].reshape(dbsz, nrow, MLA_ROPE)
    wukt = wuk2.T
    gcol = jnp.tile(g_kn[0], MLA_HEADS)[:, None]
    o_mla_s = _paged_mla_call(page_table, qn_bd, qr, lat_s.reshape(dbsz, dseq, -1),
                              kr_s[:, :MLA_ROPE].reshape(dbsz, dseq, MLA_ROPE),
                              wukt, gcol, wuv2.astype(BF16),
                              cache_mla_latent[0], cache_mla_krope[0].transpose(0, 2, 1)).reshape(n_s, -1)

    qf4 = qfa_s.reshape(dbsz, dseq, FOX_HEADS, LANES)[..., :FOX_HEAD_DIM]
    rep = FOX_HEADS // FOX_KV_HEADS
    onehot_g = jnp.asarray(np.arange(FOX_HEADS)[:, None] // rep == np.arange(FOX_KV_HEADS)[None, :], BF16)
    qbd = (qf4[:, :, :, None, :] * onehot_g[None, None, :, :, None]).reshape(dbsz, nrow, FOX_KV_HEADS * FOX_HEAD_DIM)
    kf_s = kfp_s.reshape(n_s, FOX_KV_HEADS, LANES)[:, :, :FOX_HEAD_DIM]
    cn = cum_s[:, LF0:LF0 + FOX_HEADS].reshape(dbsz, dseq, FOX_HEADS)
    cn_col = cn.reshape(dbsz, nrow, 1)
    cn_t = jnp.pad(cn.transpose(0, 2, 1), ((0, 0), (0, 0), (0, LANES - dseq)))
    kvw = FOX_KV_HEADS * FOX_HEAD_DIM
    clft = cache_fox_logf[0].transpose(0, 2, 1)
    cet = _suffix_call(clft, sc["triu"], sc["ones"])
    ckt = cache_fox_k[0].transpose(0, 2, 3, 1).reshape(n_pool, kvw, PAGE)
    cvt = cache_fox_v[0].transpose(0, 2, 3, 1).reshape(n_pool, kvw, PAGE)
    o_fox_s = _paged_fox_call(page_table, qbd, kf_s.reshape(dbsz, dseq, kvw), vf_s.reshape(dbsz, dseq, kvw),
                              cn_col, cn_t, sc["psel"], ckt, cvt, cet).reshape(n_s, -1)

    x1_p, h2_p, rt_p = _merge_call(xp2, o_mla_p, o_fox_p, sg_p, mod3, mod3, mod3,
                                   (mod_spec_p(2, tm_p), mod_spec_p(3, tm_p), mod_spec_p(4, tm_p)), consts,
                                   o_transposed=True)
    x1_s, h2_s, rt_s = _merge_call(xs2, o_mla_s, o_fox_s, sg_s, mod_s, mod_s, mod_s,
                                   (mod_spec_s(2, tm_s), mod_spec_s(3, tm_s), mod_spec_s(4, tm_s)), consts,
                                   o_transposed=False)

    h2 = jnp.concatenate([h2_p, h2_s], axis=0)
    rt = jnp.concatenate([rt_p, rt_s], axis=0)
    e_flat = rt[:, 0:TOP_K].astype(jnp.int32).reshape(-1)
    n_pairs = n_tok * TOP_K
    onehot = (e_flat[:, None] == jnp.arange(N_EXPERTS, dtype=jnp.int32)[None, :]).astype(jnp.int32)
    incl = jnp.cumsum(onehot, axis=0)
    counts = incl[-1]
    rank = jnp.sum((incl - onehot) * onehot, axis=1)
    tiles_e = (counts + TG_MOE - 1) // TG_MOE
    tile_end = jnp.cumsum(tiles_e)
    tile_start = tile_end - tiles_e
    dest = jnp.sum(onehot * tile_start[None, :], axis=1) * TG_MOE + rank
    n_tiles = n_pairs // TG_MOE + N_EXPERTS
    p_total = n_tiles * TG_MOE
    src_tok = jnp.zeros((p_total,), jnp.int32).at[dest].set(jnp.arange(n_pairs, dtype=jnp.int32) // TOP_K)
    n_valid = tile_end[-1:].astype(jnp.int32)
    tile_ids = jnp.arange(n_tiles, dtype=jnp.int32)
    tile_e = jnp.sum((jnp.minimum(tile_ids, n_valid[0] - 1)[:, None] >= tile_end[None, :]).astype(jnp.int32), axis=1)
    tile_e = jnp.minimum(tile_e, N_EXPERTS - 1).astype(jnp.int32)
    xs_sorted = jnp.take(h2, src_tok, axis=0, mode="clip")
    ys = _gmm_call(tile_e, n_valid, xs_sorted, w_e_gate[0], w_e_up[0], w_e_down[0])
    dest2 = dest.reshape(n_tok, TOP_K)
    pick = lambda lo, hi, slot: jnp.take(ys, dest2[lo:hi, slot], axis=0, mode="clip")

    y_p = _final_call(x1_p, pick(0, n_p, 0), pick(0, n_p, 1), rt_p, mod3,
                      lambda tm: pl.BlockSpec((None, 1, d), lambda i: ((i * tm) // seq, 0, 5)))
    y_s = _final_call(x1_s, pick(n_p, n_tok, 0), pick(n_p, n_tok, 1), rt_s, mod_s,
                      lambda tm: pl.BlockSpec((tm, d), lambda i: (i, 5)))

    def leaves(lat, kr, kfp, vf, lf, bb, tt):
        return (lat.reshape(1, bb, tt, MLA_KV_RANK),
                kr[:, :MLA_ROPE].reshape(1, bb, tt, MLA_ROPE),
                kfp.reshape(-1, FOX_KV_HEADS, LANES)[:, :, :FOX_HEAD_DIM].reshape(1, bb, tt, FOX_KV_HEADS, FOX_HEAD_DIM),
                vf.reshape(1, bb, tt, FOX_KV_HEADS, FOX_HEAD_DIM),
                lf[:, LF0:LF0 + FOX_HEADS].reshape(1, bb, tt, FOX_HEADS))

    return ((y_p.reshape(bsz, seq, d), y_s.reshape(dbsz, dseq, d))
            + leaves(lat_p, kr_p, kfp_p, vf_p, lf_p, bsz, seq)
            + leaves(lat_s, kr_s, kfp_s, vf_s, lf_s, dbsz, dseq))
```

```python
import functools

import numpy as np
import jax
import jax.numpy as jnp
from jax import lax
from jax.experimental import pallas as pl
from jax.experimental.pallas import tpu as pltpu

F32 = jnp.float32
BF16 = jnp.bfloat16

MLA_HEADS, MLA_NOPE, MLA_ROPE, MLA_V = 8, 64, 32, 64
MLA_Q_RANK, MLA_KV_RANK = 512, 256
FOX_HEADS, FOX_KV_HEADS, FOX_HEAD_DIM = 8, 4, 64
N_GROUPS, EXPERTS_PER_GROUP, TOP_K = 4, 8, 2
N_EXPERTS = N_GROUPS * EXPERTS_PER_GROUP
N_MOD = 6
ROPE_THETA = 10000.0
RMS_EPS = 1e-6
PAGE = 128
LANES = 128
AUG0 = 64
LF0 = 64
DV_AUG = 80

VMEM_LIMIT = 56 * 1024 * 1024
TM_PROJ = 256
TM_GATES = 512
TN_GATES = 1024
TN_MOD = 1024
TQ_ATTN = 512
PAGES_PER_CHUNK = 32
MLA_SUB_KEYS = 1024
PREPASS_PAGES = 256
TG_MOE = 256
TM_FINAL = 512


def _cparams(sem):
    return pltpu.CompilerParams(dimension_semantics=sem, vmem_limit_bytes=VMEM_LIMIT)


def _vmem_whole():
    return pl.BlockSpec(memory_space=pltpu.VMEM)


def _dot(a, b):
    return jnp.dot(a, b, preferred_element_type=F32)


def _dot_nt(a, b):
    return lax.dot_general(a, b, (((1,), (1,)), ((), ())), preferred_element_type=F32)


def _split3(x):
    hi = x.astype(BF16)
    r1 = x - hi.astype(F32)
    mid = r1.astype(BF16)
    lo = (r1 - mid.astype(F32)).astype(BF16)
    return hi, mid, lo


def _group_rsqrt(v, e_ref, inv_cnt):
    ss = _dot((v * v).astype(BF16), e_ref[...])
    return lax.rsqrt(ss * inv_cnt + RMS_EPS)


def _group_bcast(r, et_ref):
    rh = r.astype(BF16)
    rl = (r - rh.astype(F32)).astype(BF16)
    return _dot(rh, et_ref[...]) + _dot(rl, et_ref[...])


def _mod_kernel(c_ref, w_ref, b_ref, o_ref):
    c = c_ref[...]
    a = (c / (1.0 + jnp.exp(-c))).astype(BF16)
    o_ref[...] = _dot(a, w_ref[...].astype(BF16)) + b_ref[...]


def _mod_call(c_all, w_ada, b_ada):
    rows, d = c_all.shape
    n = w_ada.shape[1]
    return pl.pallas_call(
        _mod_kernel,
        out_shape=jax.ShapeDtypeStruct((rows, n), F32),
        grid=(n // TN_MOD,),
        in_specs=[pl.BlockSpec((rows, d), lambda j: (0, 0)),
                  pl.BlockSpec((d, TN_MOD), lambda j: (0, j)),
                  pl.BlockSpec((1, TN_MOD), lambda j: (0, j))],
        out_specs=pl.BlockSpec((rows, TN_MOD), lambda j: (0, j)),
        compiler_params=_cparams(("arbitrary",)),
        name="mod",
    )(c_all, w_ada, b_ada)


C_QA = (0, 512)
C_KVL = (512, 768)
C_FQ = (768, 1792)
C_FK = (1792, 2304)
C_FV = (2304, 2560)
C_MISC = (2560, 2688)
N_SMALL = 2688


def _proj_kernel(x_ref, m0_ref, m1_ref, gattn_ref, wsm_ref, wqb_ref, wkv_ref,
                 gqa_ref, gkva_ref, gq_ref, gkr_ref, gfq_ref, gfk_ref, gkn_ref, bf_ref,
                 tab_ref, eq_ref, eqt_ref, cntq_ref, ef_ref, eft_ref, ek_ref, ekt_ref,
                 pkr_ref, pq_ref, pk_ref, onesq_ref, onesk_ref, wuvt_ref, onesm_ref, wfvt_ref, onesf_ref,
                 tri_ref,
                 hb_ref, qmla_ref, lat_ref, kr_ref, qfa_ref, kfp_ref, kfa_ref, vf_ref,
                 lf_ref, cum_ref, kmla_ref, vmt_ref, vft_ref, carry_ref, *, tiles_per_seq, with_kv):
    i = pl.program_id(0)
    x = x_ref[...]
    xn = x * lax.rsqrt(jnp.mean(x * x, axis=-1, keepdims=True) + RMS_EPS) * gattn_ref[...]
    h = xn * (1.0 + m1_ref[...]) + m0_ref[...]
    hb = h.astype(BF16)
    hb_ref[...] = hb
    z = _dot(hb, wsm_ref[...])
    qa = z[:, C_QA[0]:C_QA[1]]
    kvl = z[:, C_KVL[0]:C_KVL[1]]
    fq = z[:, C_FQ[0]:C_FQ[1]]
    fk = z[:, C_FK[0]:C_FK[1]]
    fv = z[:, C_FV[0]:C_FV[1]]
    misc = z[:, C_MISC[0]:C_MISC[1]]
    cosq, sinq, cosk, sink = tab_ref[0], tab_ref[1], tab_ref[2], tab_ref[3]

    qan = qa * lax.rsqrt(jnp.mean(qa * qa, axis=-1, keepdims=True) + RMS_EPS) * gqa_ref[...]
    qq = _dot(qan.astype(BF16), wqb_ref[...])
    nq = MLA_HEADS * LANES
    q, qsw = qq[:, :nq], qq[:, nq:]
    sc = _group_bcast(_group_rsqrt(q, eq_ref, cntq_ref[...]), eqt_ref)
    gq = gq_ref[...]
    for hh in range(MLA_HEADS):
        sl = slice(hh * LANES, (hh + 1) * LANES)
        qo = sc[:, sl] * (q[:, sl] * (gq[0:1, sl] * cosq) + qsw[:, sl] * (gq[1:2, sl] * sinq))
        qmla_ref[:, sl] = qo.astype(BF16)

    lat = kvl * lax.rsqrt(jnp.mean(kvl * kvl, axis=-1, keepdims=True) + RMS_EPS) * gkva_ref[...]
    lat_ref[...] = lat

    lane = lax.broadcasted_iota(jnp.int32, misc.shape, 1)
    kr2 = jnp.where(lane < MLA_ROPE, misc * misc, 0.0)
    rk = lax.rsqrt(jnp.sum(kr2, axis=-1, keepdims=True) * (1.0 / MLA_ROPE) + RMS_EPS)
    msw = pltpu.roll(misc, LANES - MLA_ROPE, 1)
    gkr = gkr_ref[...]
    kr = rk * (misc * (gkr[0:1] * cosk) + msw * (gkr[1:2] * sink))
    kr_ref[...] = kr

    xl = misc + bf_ref[...]
    lf = jnp.minimum(xl, 0.0) - jnp.log1p(jnp.exp(-jnp.abs(xl)))
    lf = jnp.where((lane >= LF0) & (lane < LF0 + FOX_HEADS), lf, 0.0)
    lf_ref[...] = lf
    l_hi, l_mid, l_lo = _split3(lf)
    tri = tri_ref[...]
    cum = _dot(tri, l_hi) + _dot(tri, l_mid) + _dot(tri, l_lo)
    if tiles_per_seq > 1:
        @pl.when(i % tiles_per_seq == 0)
        def _():
            carry_ref[...] = jnp.zeros_like(carry_ref)
        cum = cum + carry_ref[...]
        carry_ref[...] = cum[cum.shape[0] - 1:, :]
    cum_ref[...] = cum
    c_hi, c_mid, c_lo = _split3(cum)
    cs = jnp.concatenate([c_hi, c_mid, c_lo], axis=1)
    augq = _dot(cs, pq_ref[...]) + onesq_ref[...]
    augk = _dot(cs, pk_ref[...]) + onesk_ref[...]

    scf = _group_bcast(_group_rsqrt(fq, ef_ref, 1.0 / FOX_HEAD_DIM), eft_ref)
    qfa_ref[...] = (fq * scf * gfq_ref[...] + augq).astype(BF16)
    sck = _group_bcast(_group_rsqrt(fk, ek_ref, 1.0 / FOX_HEAD_DIM), ekt_ref)
    kf = fk * sck * gfk_ref[...]
    kfp_ref[...] = kf
    kfa_ref[...] = (kf + augk).astype(BF16)
    vf_ref[...] = fv

    if with_kv:
        lat_b = lat.astype(BF16)
        kn = _dot(lat_b, wkv_ref[...])
        scn = _group_bcast(_group_rsqrt(kn, ef_ref, 1.0 / MLA_NOPE), eft_ref)
        kp = _dot(kr.astype(BF16), pkr_ref[...])
        kmla_ref[...] = (kn * scn * gkn_ref[...] + kp).astype(BF16)
        vmt_ref[...] = (_dot_nt(wuvt_ref[...], lat_b) + onesm_ref[...]).astype(BF16)
        vft_ref[...] = (_dot_nt(wfvt_ref[...], hb) + onesf_ref[...]).astype(BF16)
    else:
        kmla_ref[...] = jnp.zeros_like(kmla_ref)
        vmt_ref[...] = jnp.zeros_like(vmt_ref)
        vft_ref[...] = jnp.zeros_like(vft_ref)


def _proj_call(x2d, m0, m1, mod_specs, tabs, tri, consts, *, seq_rows, with_kv):
    n, d = x2d.shape
    tm = min(TM_PROJ, n)
    tiles_per_seq = max(seq_rows // tm, 1)
    n_tab_tiles = tabs.shape[1] // tm
    row = lambda c: pl.BlockSpec((tm, c), lambda i: (i, 0))
    in_specs = [row(d), mod_specs[0], mod_specs[1]] + [_vmem_whole()] * 12
    in_specs += [pl.BlockSpec((4, tm, LANES), lambda i: (0, i % n_tab_tiles, 0))]
    in_specs += [_vmem_whole()] * 16
    in_specs += [pl.BlockSpec((tm, tm), lambda i: (0, 0))]
    nq = MLA_HEADS * LANES
    out_cols = [(d, BF16), (nq, BF16), (MLA_KV_RANK, F32), (LANES, F32), (FOX_HEADS * LANES, BF16),
                (FOX_KV_HEADS * LANES, F32), (FOX_KV_HEADS * LANES, BF16),
                (FOX_KV_HEADS * FOX_HEAD_DIM, F32),
                (LANES, F32), (LANES, F32), (nq, BF16)]
    out_rows_t = [MLA_HEADS * DV_AUG, FOX_KV_HEADS * DV_AUG]
    col = lambda r: pl.BlockSpec((r, tm), lambda i: (0, i))
    return pl.pallas_call(
        functools.partial(_proj_kernel, tiles_per_seq=tiles_per_seq, with_kv=with_kv),
        out_shape=([jax.ShapeDtypeStruct((n, c), dt) for c, dt in out_cols]
                   + [jax.ShapeDtypeStruct((r, n), BF16) for r in out_rows_t]),
        grid=(n // tm,),
        in_specs=in_specs,
        out_specs=[row(c) for c, _ in out_cols] + [col(r) for r in out_rows_t],
        scratch_shapes=[pltpu.VMEM((1, LANES), F32)],
        compiler_params=_cparams(("arbitrary",)),
        name="proj",
    )(x2d, m0, m1, *consts["proj_a"], tabs, *consts["proj_b"], tri)


def _gates_kernel(h_ref, w_ref, o_ref):
    z = _dot(h_ref[...], w_ref[...])
    o_ref[...] = (1.0 / (1.0 + jnp.exp(-z))).astype(BF16)


def _gates_call(hb, wg):
    n, d = hb.shape
    nc = wg.shape[1]
    tm = min(TM_GATES, n)
    return pl.pallas_call(
        _gates_kernel,
        out_shape=jax.ShapeDtypeStruct((n, nc), BF16),
        grid=(nc // TN_GATES, n // tm),
        in_specs=[pl.BlockSpec((tm, d), lambda j, i: (i, 0)),
                  pl.BlockSpec((d, TN_GATES), lambda j, i: (0, j))],
        out_specs=pl.BlockSpec((tm, TN_GATES), lambda j, i: (i, j)),
        compiler_params=_cparams(("arbitrary", "arbitrary")),
        name="gates",
    )(hb, wg)


def _attn_kernel(ii_ref, jj_ref, q_ref, k_ref, vt_ref, o_ref, m_ref, acc_ref, *, nkv, rep, dv):
    i = ii_ref[pl.program_id(1)]
    j = jj_ref[pl.program_id(1)]
    tq = q_ref.shape[1]

    @pl.when(j == 0)
    def _():
        m_ref[...] = jnp.full_like(m_ref, -jnp.inf)
        acc_ref[...] = jnp.zeros_like(acc_ref)

    def update(masked):
        for g in range(nkv):
            k = k_ref[0, :, g * LANES:(g + 1) * LANES]
            qs = [q_ref[0, :, (g * rep + e) * LANES:(g * rep + e + 1) * LANES] for e in range(rep)]
            q = qs[0] if rep == 1 else jnp.concatenate(qs, axis=0)
            s = _dot_nt(k, q)
            if masked:
                rows = lax.broadcasted_iota(jnp.int32, s.shape, 0)
                cols = lax.broadcasted_iota(jnp.int32, s.shape, 1)
                s = jnp.where(rows <= cols % tq, s, -jnp.inf)
            m_prev = m_ref[g]
            m_new = jnp.maximum(m_prev, jnp.max(s, axis=0, keepdims=True))
            alpha = jnp.exp(m_prev - m_new)
            p = jnp.exp(s - m_new).astype(BF16)
            acc_ref[g] = alpha * acc_ref[g] + _dot(vt_ref[g * DV_AUG:(g + 1) * DV_AUG, :], p)
            m_ref[g] = m_new

    @pl.when(j < i)
    def _():
        update(False)

    @pl.when(j == i)
    def _():
        update(True)
        for g in range(nkv):
            for e in range(rep):
                a = acc_ref[g][:, e * tq:(e + 1) * tq]
                h = g * rep + e
                o_ref[h * dv:(h + 1) * dv, :] = (a[0:dv] / a[dv:dv + 1]).astype(BF16)


def _attn_call(q, k, vt, *, nkv, rep, dv, name):
    b, t, _ = q.shape
    tq = min(TQ_ATTN, t)
    nq = t // tq
    nh = nkv * rep
    pairs = [(i, j) for i in range(nq) for j in range(i + 1)]
    ii = jnp.asarray([p[0] for p in pairs], jnp.int32)
    jj = jnp.asarray([p[1] for p in pairs], jnp.int32)
    grid_spec = pltpu.PrefetchScalarGridSpec(
        num_scalar_prefetch=2, grid=(b, len(pairs)),
        in_specs=[pl.BlockSpec((1, tq, q.shape[2]), lambda bb, p, ii_, jj_: (bb, ii_[p], 0)),
                  pl.BlockSpec((1, tq, k.shape[2]), lambda bb, p, ii_, jj_: (bb, jj_[p], 0)),
                  pl.BlockSpec((nkv * DV_AUG, tq), lambda bb, p, ii_, jj_: (0, bb * nq + jj_[p]))],
        out_specs=pl.BlockSpec((nh * dv, tq), lambda bb, p, ii_, jj_: (0, bb * nq + ii_[p])),
        scratch_shapes=[pltpu.VMEM((nkv, 1, rep * tq), F32), pltpu.VMEM((nkv, DV_AUG, rep * tq), F32)])
    return pl.pallas_call(
        functools.partial(_attn_kernel, nkv=nkv, rep=rep, dv=dv),
        out_shape=jax.ShapeDtypeStruct((nh * dv, b * t), BF16),
        grid_spec=grid_spec,
        compiler_params=_cparams(("arbitrary", "arbitrary")),
        name=name,
    )(ii, jj, q, k, vt)


def _softmax_update(s, pv, m_ref, l_ref, acc_ref):
    m_prev = m_ref[...]
    m_new = jnp.maximum(m_prev, jnp.max(s, axis=-1, keepdims=True))
    alpha = jnp.exp(m_prev - m_new)
    p = jnp.exp(s - m_new)
    l_ref[...] = alpha * l_ref[...] + jnp.sum(p, axis=-1, keepdims=True)
    acc_ref[...] = alpha * acc_ref[...] + pv(p.astype(BF16))
    m_ref[...] = m_new


def _chunk_loop(b, n_seq, n_chunks, start, wait, compute):
    @pl.when(b == 0)
    def _():
        start(b, 0, 0)

    def pair(c2, carry):
        c = 2 * c2
        start(b, c + 1, 1)
        wait(0)
        compute(0)

        @pl.when(c + 2 < n_chunks)
        def _():
            start(b, c + 2, 0)

        @pl.when(jnp.logical_and(c + 2 >= n_chunks, b + 1 < n_seq))
        def _():
            start(b + 1, 0, 0)

        wait(1)
        compute(1)
        return carry

    lax.fori_loop(0, n_chunks // 2, pair, 0)


def _head_select(full, dv):
    rows = lax.broadcasted_iota(jnp.int32, full.shape, 0)
    cols = lax.broadcasted_iota(jnp.int32, full.shape, 1)
    nh = full.shape[1] // dv
    sel = jnp.where(cols // dv == rows % nh, full, 0.0)
    return jnp.sum(sel.reshape(full.shape[0] // nh, nh, full.shape[1]), axis=1)


class _Branch:
    def __init__(self, copies, init_once, init_seq, compute, finish):
        self.copies, self.init_once, self.init_seq, self.compute, self.finish = (
            copies, init_once, init_seq, compute, finish)


def _mla_branch(pt_ref, qn_ref, qr_ref, latn_ref, krn_ref, wukt_ref, gcol_ref, wuv_ref,
                clat_ref, ckr_ref, o_ref,
                a_ref, wg_ref, latbuf, krbuf, newlat, newkr, sem, m_ref, l_ref, acc_ref,
                *, n_chunks, pages):
    ck = pages * PAGE
    sub = min(MLA_SUB_KEYS, ck)
    nrow = qn_ref.shape[1]
    nup = wukt_ref.shape[0]
    t_new = latn_ref.shape[1]

    def copies(seq, k, slot):
        out = []
        for p in range(pages):
            pid = pt_ref[seq, k * pages + p]
            out.append(pltpu.make_async_copy(clat_ref.at[pid], latbuf.at[slot, pl.ds(p * PAGE, PAGE)],
                                             sem.at[0, slot]))
            out.append(pltpu.make_async_copy(ckr_ref.at[pid], krbuf.at[slot, :, pl.ds(p * PAGE, PAGE)],
                                             sem.at[1, slot]))
        return out

    def init_once():
        w = wukt_ref[...]
        a_ref[0:nup, :] = w.astype(BF16)
        wg_ref[...] = (w * gcol_ref[...]).astype(BF16)
        newlat[...] = jnp.zeros_like(newlat)
        newkr[...] = jnp.zeros_like(newkr)

    def init_seq():
        a_ref[nup:nup + nrow, :] = _dot(qn_ref[0], wg_ref[...]).astype(BF16)
        m_ref[...] = jnp.full_like(m_ref, -jnp.inf)
        l_ref[...] = jnp.zeros_like(l_ref)
        acc_ref[...] = jnp.zeros_like(acc_ref)

    def nope_scores(lat_b):
        r_all = _dot_nt(a_ref[...], lat_b)
        kup = r_all[0:nup]
        nk = kup.shape[1]
        ss = jnp.sum((kup * kup).reshape(MLA_HEADS, MLA_NOPE, nk), axis=1)
        rn = lax.rsqrt(ss * (1.0 / MLA_NOPE) + RMS_EPS)
        sn = r_all[nup:nup + nrow].reshape(nrow // MLA_HEADS, MLA_HEADS, nk) * rn[None]
        return sn.reshape(nrow, nk)

    def compute(slot):
        lat_b = latbuf[slot].astype(BF16)
        parts = [nope_scores(lat_b[j * sub:(j + 1) * sub]) for j in range(ck // sub)]
        s = jnp.concatenate(parts, axis=1) + _dot(qr_ref[0], krbuf[slot].astype(BF16))
        _softmax_update(s, lambda p: _dot(p, lat_b), m_ref, l_ref, acc_ref)

    def finish():
        newlat[0:t_new, :] = latn_ref[0]
        newkr[0:t_new, :] = krn_ref[0]
        lat_n = newlat[...].astype(BF16)
        s = nope_scores(lat_n) + _dot_nt(qr_ref[0], newkr[...].astype(BF16))
        rows = lax.broadcasted_iota(jnp.int32, s.shape, 0)
        cols = lax.broadcasted_iota(jnp.int32, s.shape, 1)
        s = jnp.where(cols <= rows // MLA_HEADS, s, -jnp.inf)
        _softmax_update(s, lambda p: _dot(p, lat_n), m_ref, l_ref, acc_ref)
        o_lat = (acc_ref[...] / l_ref[...]).astype(BF16)
        o_ref[0] = _head_select(_dot(o_lat, wuv_ref[...]), MLA_V).astype(BF16)

    return _Branch(copies, init_once, init_seq, compute, finish)


def _mla_scratch(ck, nrow, nup):
    return [pltpu.VMEM((nup + nrow, MLA_KV_RANK), BF16),
            pltpu.VMEM((nup, MLA_KV_RANK), BF16),
            pltpu.VMEM((2, ck, MLA_KV_RANK), F32),
            pltpu.VMEM((2, MLA_ROPE, ck), F32),
            pltpu.VMEM((LANES, MLA_KV_RANK), F32),
            pltpu.VMEM((LANES, MLA_ROPE), F32),
            pltpu.SemaphoreType.DMA((2, 2)),
            pltpu.VMEM((nrow, 1), F32), pltpu.VMEM((nrow, 1), F32),
            pltpu.VMEM((nrow, MLA_KV_RANK), F32)]


def _suffix_kernel(x_ref, triu_ref, ones_ref, o_ref):
    npg, nh, pg = x_ref.shape
    x = x_ref[...].reshape(npg * nh, pg)
    e = jnp.zeros(x.shape, F32)
    t = jnp.zeros(x.shape, F32)
    for piece in _split3(x):
        e = e + _dot(piece, triu_ref[...])
        t = t + _dot(piece, ones_ref[...])
    o_ref[:, 0:nh, :] = e.reshape(npg, nh, pg)
    o_ref[:, nh:2 * nh, :] = t.reshape(npg, nh, pg)


def _suffix_call(lft, triu, ones):
    n_pool, nh, pg = lft.shape
    npg = min(PREPASS_PAGES, n_pool)
    assert n_pool % npg == 0
    return pl.pallas_call(
        _suffix_kernel,
        out_shape=jax.ShapeDtypeStruct((n_pool, 2 * nh, pg), F32),
        grid=(n_pool // npg,),
        in_specs=[pl.BlockSpec((npg, nh, pg), lambda i: (i, 0, 0)), _vmem_whole(), _vmem_whole()],
        out_specs=pl.BlockSpec((npg, 2 * nh, pg), lambda i: (i, 0, 0)),
        compiler_params=_cparams(("arbitrary",)),
        name="suffix",
    )(lft, triu, ones)


def _fox_branch(pt_ref, qbd_ref, kn_ref, vn_ref, cncol_ref, cnt_ref, psel_ref,
                ck_ref, cv_ref, cet_ref, o_ref,
                kbuf, vbuf, etbuf, newk, newv, sem, m_ref, l_ref, acc_ref, sufc_ref,
                *, n_chunks, pages):
    ck = pages * PAGE
    nrow = qbd_ref.shape[1]
    t_new = kn_ref.shape[1]
    nt = nrow // FOX_HEADS

    def copies(seq, k, slot):
        out = []
        kk = n_chunks - 1 - k
        for p in range(pages):
            pid = pt_ref[seq, kk * pages + p]
            dst = pl.ds(p * PAGE, PAGE)
            out.append(pltpu.make_async_copy(ck_ref.at[pid], kbuf.at[slot, :, dst], sem.at[0, slot]))
            out.append(pltpu.make_async_copy(cv_ref.at[pid], vbuf.at[slot, :, dst], sem.at[1, slot]))
            out.append(pltpu.make_async_copy(cet_ref.at[pid], etbuf.at[slot, :, dst], sem.at[2, slot]))
        return out

    def init_once():
        newk[...] = jnp.zeros_like(newk)
        newv[...] = jnp.zeros_like(newv)

    def init_seq():
        m_ref[...] = jnp.full_like(m_ref, -jnp.inf)
        l_ref[...] = jnp.zeros_like(l_ref)
        acc_ref[...] = jnp.zeros_like(acc_ref)
        sufc_ref[...] = jnp.zeros_like(sufc_ref)

    def compute(slot):
        cncol = cncol_ref[0]
        s = _dot(qbd_ref[0], kbuf[slot].astype(BF16))
        carry = sufc_ref[...]
        pieces = [None] * pages
        for p in reversed(range(pages)):
            lanes_p = slice(p * PAGE, (p + 1) * PAGE)
            pieces[p] = etbuf[slot, 0:FOX_HEADS, lanes_p] + carry
            carry = carry + etbuf[slot, FOX_HEADS:2 * FOX_HEADS, lanes_p]
        sufc_ref[...] = carry
        bias = jnp.concatenate(pieces, axis=1)
        s = (s.reshape(nt, FOX_HEADS, ck) + bias[None]).reshape(nrow, ck) + cncol
        v_b = vbuf[slot].astype(BF16)
        _softmax_update(s, lambda p: _dot_nt(p, v_b), m_ref, l_ref, acc_ref)

    def finish():
        newk[0:t_new, :] = kn_ref[0]
        newv[0:t_new, :] = vn_ref[0]
        s = _dot_nt(qbd_ref[0], newk[...].astype(BF16))
        s = (s.reshape(nt, FOX_HEADS, LANES) - cnt_ref[0][None]).reshape(nrow, LANES) + cncol_ref[0]
        rows = lax.broadcasted_iota(jnp.int32, s.shape, 0)
        cols = lax.broadcasted_iota(jnp.int32, s.shape, 1)
        s = jnp.where(cols <= rows // FOX_HEADS, s, -jnp.inf)
        v_n = newv[...].astype(BF16)
        _softmax_update(s, lambda p: _dot(p, v_n), m_ref, l_ref, acc_ref)
        o = (acc_ref[...] / l_ref[...]).astype(BF16)
        o_ref[0] = _head_select(_dot(o, psel_ref[...]), FOX_HEAD_DIM).astype(BF16)

    return _Branch(copies, init_once, init_seq, compute, finish)


def _fox_scratch(ck, nrow):
    kvw = FOX_KV_HEADS * FOX_HEAD_DIM
    return [pltpu.VMEM((2, kvw, ck), F32), pltpu.VMEM((2, kvw, ck), F32),
            pltpu.VMEM((2, 2 * FOX_HEADS, ck), F32),
            pltpu.VMEM((LANES, kvw), F32), pltpu.VMEM((LANES, kvw), F32),
            pltpu.SemaphoreType.DMA((3, 2)),
            pltpu.VMEM((nrow, 1), F32), pltpu.VMEM((nrow, 1), F32),
            pltpu.VMEM((nrow, kvw), F32), pltpu.VMEM((FOX_HEADS, LANES), F32)]


N_MLA_IN, N_FOX_IN = 9, 9
N_MLA_SCRATCH, N_FOX_SCRATCH = 10, 10


def _paged_kernel(pt_ref, *refs, n_seq, n_chunks, pages):
    ins = refs[:N_MLA_IN + N_FOX_IN]
    o_mla, o_fox = refs[N_MLA_IN + N_FOX_IN:N_MLA_IN + N_FOX_IN + 2]
    scratch = refs[N_MLA_IN + N_FOX_IN + 2:]
    mla = _mla_branch(pt_ref, *ins[:N_MLA_IN], o_mla, *scratch[:N_MLA_SCRATCH], n_chunks=n_chunks, pages=pages)
    fox = _fox_branch(pt_ref, *ins[N_MLA_IN:], o_fox, *scratch[N_MLA_SCRATCH:], n_chunks=n_chunks, pages=pages)
    b = pl.program_id(0)

    def start(seq, k, slot):
        for c in mla.copies(seq, k, slot) + fox.copies(seq, k, slot):
            c.start()

    def wait(slot):
        for c in mla.copies(0, 0, slot) + fox.copies(0, 0, slot):
            c.wait()

    def compute(slot):
        mla.compute(slot)
        fox.compute(slot)

    @pl.when(b == 0)
    def _():
        mla.init_once()
        fox.init_once()

    mla.init_seq()
    fox.init_seq()
    _chunk_loop(b, n_seq, n_chunks, start, wait, compute)
    mla.finish()
    fox.finish()


def _paged_call(pt, mla_in, fox_in):
    n_seq, n_pages = pt.shape
    pages = min(PAGES_PER_CHUNK, n_pages // 2)
    n_chunks = n_pages // pages
    ck = pages * PAGE
    nrow = mla_in[0].shape[1]
    t_new = mla_in[2].shape[1]
    seq3 = lambda a: pl.BlockSpec((1,) + a.shape[1:], lambda b, pt_: (b, 0, 0))
    anyspec = pl.BlockSpec(memory_space=pl.ANY)
    whole = _vmem_whole()
    in_specs = ([seq3(a) for a in mla_in[:4]] + [whole, whole, whole, anyspec, anyspec]
                + [seq3(a) for a in fox_in[:5]] + [whole, anyspec, anyspec, anyspec])
    o_spec = lambda w: pl.BlockSpec((1, t_new, w), lambda b, pt_: (b, 0, 0))
    wm, wf = MLA_HEADS * MLA_V, FOX_HEADS * FOX_HEAD_DIM
    grid_spec = pltpu.PrefetchScalarGridSpec(
        num_scalar_prefetch=1, grid=(n_seq,),
        in_specs=in_specs,
        out_specs=[o_spec(wm), o_spec(wf)],
        scratch_shapes=_mla_scratch(ck, nrow, mla_in[4].shape[0]) + _fox_scratch(ck, nrow))
    return pl.pallas_call(
        functools.partial(_paged_kernel, n_seq=n_seq, n_chunks=n_chunks, pages=pages),
        out_shape=[jax.ShapeDtypeStruct((n_seq, t_new, wm), BF16), jax.ShapeDtypeStruct((n_seq, t_new, wf), BF16)],
        grid_spec=grid_spec,
        compiler_params=_cparams(("arbitrary",)),
        name="paged",
    )(pt, *mla_in, *fox_in)


def _merge_kernel(x_ref, om_ref, of_ref, sg_ref, m2_ref, m3_ref, m4_ref, gffn_ref,
                  wba_ref, wbb_ref, wo_ref, wrh_ref, wrl_ref, br_ref,
                  x1_ref, h2_ref, rt_ref, *, o_transposed):
    d = x_ref.shape[1]
    if o_transposed:
        tn = lambda o, w: lax.dot_general(o, w, (((0,), (0,)), ((), ())), preferred_element_type=F32)
        a = tn(om_ref[...], wba_ref[...])
        bb = tn(of_ref[...], wbb_ref[...])
    else:
        a = _dot(om_ref[...], wba_ref[...])
        bb = _dot(of_ref[...], wbb_ref[...])
    u = sg_ref[:, 0:d].astype(F32) * a + sg_ref[:, d:2 * d].astype(F32) * bb
    x1 = x_ref[...] + m2_ref[...] * _dot(u.astype(BF16), wo_ref[...])
    x1_ref[...] = x1
    xn = x1 * lax.rsqrt(jnp.mean(x1 * x1, axis=-1, keepdims=True) + RMS_EPS) * gffn_ref[...]
    h2 = xn * (1.0 + m4_ref[...]) + m3_ref[...]
    h2_ref[...] = h2
    h2_hi = h2.astype(BF16)
    h2_lo = (h2 - h2_hi.astype(F32)).astype(BF16)
    lg = (_dot(h2_hi, wrh_ref[...]) + _dot(h2_hi, wrl_ref[...]) + _dot(h2_lo, wrh_ref[...])
          + br_ref[...])
    lane = lax.broadcasted_iota(jnp.int32, lg.shape, 1)
    neg = -jnp.inf
    is_g = (lane >= N_EXPERTS) & (lane < N_EXPERTS + N_GROUPS)
    lgm = jnp.where(is_g, lg, neg)
    mg = jnp.max(lgm, axis=-1, keepdims=True)
    eg = jnp.exp(lgm - mg)
    pgrp = eg / jnp.sum(eg, axis=-1, keepdims=True)
    pg_top = jnp.max(pgrp, axis=-1, keepdims=True)
    g_idx = jnp.min(jnp.where(is_g & (pgrp == pg_top), lane - N_EXPERTS, N_GROUPS), axis=-1, keepdims=True)
    sel = (lane < N_EXPERTS) & (lane // EXPERTS_PER_GROUP == g_idx)
    em = jnp.where(sel, lg, neg)
    me = jnp.max(em, axis=-1, keepdims=True)
    ee = jnp.exp(em - me)
    pe = ee / jnp.sum(ee, axis=-1, keepdims=True)
    big = 4 * LANES
    p1 = jnp.max(jnp.where(sel, pe, -1.0), axis=-1, keepdims=True)
    i1 = jnp.min(jnp.where(sel & (pe == p1), lane, big), axis=-1, keepdims=True)
    sel2 = sel & (lane != i1)
    p2 = jnp.max(jnp.where(sel2, pe, -1.0), axis=-1, keepdims=True)
    i2 = jnp.min(jnp.where(sel2 & (pe == p2), lane, big), axis=-1, keepdims=True)
    den = p1 + p2
    w1 = pg_top * p1 / den
    w2 = pg_top * p2 / den
    rt = jnp.where(lane == 0, i1.astype(F32),
                   jnp.where(lane == 1, i2.astype(F32),
                             jnp.where(lane == 2, w1, jnp.where(lane == 3, w2, 0.0))))
    rt_ref[...] = rt


def _merge_call(x2d, o_mla, o_fox, sg, m2, m3, m4, mod_specs, consts, *, o_transposed):
    n, d = x2d.shape
    tm = min(TM_PROJ, n)
    row = lambda c: pl.BlockSpec((tm, c), lambda i: (i, 0))
    col = lambda r: pl.BlockSpec((r, tm), lambda i: (0, i))
    o_spec = (lambda o: col(o.shape[0])) if o_transposed else (lambda o: row(o.shape[1]))
    in_specs = [row(d), o_spec(o_mla), o_spec(o_fox), row(sg.shape[1]),
                mod_specs[0], mod_specs[1], mod_specs[2]] + [_vmem_whole()] * 7
    return pl.pallas_call(
        functools.partial(_merge_kernel, o_transposed=o_transposed),
        out_shape=[jax.ShapeDtypeStruct((n, d), F32), jax.ShapeDtypeStruct((n, d), F32),
                   jax.ShapeDtypeStruct((n, LANES), F32)],
        grid=(n // tm,),
        in_specs=in_specs,
        out_specs=[row(d), row(d), row(LANES)],
        compiler_params=_cparams(("arbitrary",)),
        name="merge",
    )(x2d, o_mla, o_fox, sg, m2, m3, m4, *consts["merge"])


def _gmm_kernel(te_ref, nv_ref, idx_ref, idxn_ref, h2_ref, wg_ref, wu_ref, wd_ref, o_ref,
                xbuf, wgb, wub, wdb, sem):
    i = pl.program_id(0)
    nv = nv_ref[0]
    valid = i < nv
    slot = i % 2
    rows = xbuf.shape[1]

    def row_copies(src_ref, s):
        return [pltpu.make_async_copy(h2_ref.at[pl.ds(src_ref[0, 0, r], 1)], xbuf.at[s, pl.ds(r, 1)], sem.at[s])
                for r in range(rows)]

    @pl.when(i == 0)
    def _():
        for c in row_copies(idx_ref, slot):
            c.start()

    @pl.when(valid)
    def _():
        for c in row_copies(idx_ref, slot):
            c.wait()

    @pl.when(i + 1 < nv)
    def _():
        for c in row_copies(idxn_ref, 1 - slot):
            c.start()

    prev = te_ref[jnp.maximum(i - 1, 0)]
    changed = jnp.logical_or(i == 0, te_ref[i] != prev)

    @pl.when(jnp.logical_and(valid, changed))
    def _():
        wgb[...] = wg_ref[0].astype(BF16)
        wub[...] = wu_ref[0].astype(BF16)
        wdb[...] = wd_ref[0].astype(BF16)

    @pl.when(valid)
    def _():
        x = xbuf[slot].astype(BF16)
        g = _dot(x, wgb[...])
        u = _dot(x, wub[...])
        hid = (g / (1.0 + jnp.exp(-g))) * u
        o_ref[...] = _dot(hid.astype(BF16), wdb[...]).astype(BF16)

    @pl.when(jnp.logical_not(valid))
    def _():
        o_ref[...] = jnp.zeros_like(o_ref)


def _gmm_call(tile_e, n_valid, src_tok, h2, w_gate, w_up, w_down):
    n_tiles = src_tok.shape[0]
    d = h2.shape[1]
    de = w_gate.shape[2]
    idx_spec = lambda f: pl.BlockSpec((1, 1, TG_MOE), f, memory_space=pltpu.SMEM)
    grid_spec = pltpu.PrefetchScalarGridSpec(
        num_scalar_prefetch=2, grid=(n_tiles,),
        in_specs=[idx_spec(lambda i, te, nv: (i, 0, 0)),
                  idx_spec(lambda i, te, nv: (jnp.minimum(i + 1, n_tiles - 1), 0, 0)),
                  pl.BlockSpec(memory_space=pl.ANY),
                  pl.BlockSpec((1, d, de), lambda i, te, nv: (te[i], 0, 0)),
                  pl.BlockSpec((1, d, de), lambda i, te, nv: (te[i], 0, 0)),
                  pl.BlockSpec((1, de, d), lambda i, te, nv: (te[i], 0, 0))],
        out_specs=pl.BlockSpec((TG_MOE, d), lambda i, te, nv: (i, 0)),
        scratch_shapes=[pltpu.VMEM((2, TG_MOE, d), F32),
                        pltpu.VMEM((d, de), BF16), pltpu.VMEM((d, de), BF16), pltpu.VMEM((de, d), BF16),
                        pltpu.SemaphoreType.DMA((2,))])
    return pl.pallas_call(
        _gmm_kernel,
        out_shape=jax.ShapeDtypeStruct((n_tiles * TG_MOE, d), BF16),
        grid_spec=grid_spec,
        compiler_params=_cparams(("arbitrary",)),
        name="gmm",
    )(tile_e, n_valid, src_tok, src_tok, h2, w_gate, w_up, w_down)


def _final_kernel(x1_ref, ya_ref, yb_ref, rt_ref, m5_ref, o_ref):
    rt = rt_ref[...]
    ffn = rt[:, 2:3] * ya_ref[...].astype(F32) + rt[:, 3:4] * yb_ref[...].astype(F32)
    o_ref[...] = x1_ref[...] + m5_ref[...] * ffn


def _final_call(x1, ya, yb, rt, m5, m5_spec_fn):
    n, d = x1.shape
    tm = min(TM_FINAL, n)
    row = lambda c: pl.BlockSpec((tm, c), lambda i: (i, 0))
    return pl.pallas_call(
        _final_kernel,
        out_shape=jax.ShapeDtypeStruct((n, d), F32),
        grid=(n // tm,),
        in_specs=[row(d), row(d), row(d), row(LANES), m5_spec_fn(tm)],
        out_specs=row(d),
        compiler_params=_cparams(("arbitrary",)),
        name="final",
    )(x1, ya, yb, rt, m5)


def _np_indicator(n_rows, groups):
    e = np.zeros((n_rows, LANES), np.float32)
    for col, (lo, hi) in enumerate(groups):
        e[lo:hi, col] = 1.0
    return e


def _static_consts():
    c = {}
    nq = MLA_HEADS * LANES
    eq = _np_indicator(nq, [(h * LANES, h * LANES + MLA_NOPE) for h in range(MLA_HEADS)]
                       + [(h * LANES + MLA_NOPE, h * LANES + MLA_NOPE + MLA_ROPE) for h in range(MLA_HEADS)])
    cnt = np.ones((1, LANES), np.float32)
    cnt[0, :MLA_HEADS] = 1.0 / MLA_NOPE
    cnt[0, MLA_HEADS:2 * MLA_HEADS] = 1.0 / MLA_ROPE
    ef = _np_indicator(nq, [(h * LANES, h * LANES + FOX_HEAD_DIM) for h in range(FOX_HEADS)])
    ek = _np_indicator(FOX_KV_HEADS * LANES, [(g * LANES, g * LANES + FOX_HEAD_DIM) for g in range(FOX_KV_HEADS)])
    pkr = np.zeros((LANES, nq), np.float32)
    for h in range(MLA_HEADS):
        for j in range(MLA_ROPE):
            pkr[j, h * LANES + MLA_NOPE + j] = 1.0
    pq = np.zeros((3 * LANES, FOX_HEADS * LANES), np.float32)
    pk = np.zeros((3 * LANES, FOX_KV_HEADS * LANES), np.float32)
    onesq = np.zeros((1, FOX_HEADS * LANES), np.float32)
    onesk = np.zeros((1, FOX_KV_HEADS * LANES), np.float32)
    rep = FOX_HEADS // FOX_KV_HEADS
    for h in range(FOX_HEADS):
        g, e = h // rep, h % rep
        for piece in range(3):
            pq[piece * LANES + LF0 + h, h * LANES + AUG0 + piece] = 1.0
            pk[piece * LANES + LF0 + h, g * LANES + AUG0 + 3 + 3 * e + piece] = -1.0
            onesq[0, h * LANES + AUG0 + 3 + 3 * e + piece] = 1.0
    for g in range(FOX_KV_HEADS):
        onesk[0, g * LANES + AUG0:g * LANES + AUG0 + 3] = 1.0
    psel = np.zeros((FOX_KV_HEADS * FOX_HEAD_DIM, FOX_HEADS * FOX_HEAD_DIM), np.float32)
    for h in range(FOX_HEADS):
        g = h // rep
        for dd in range(FOX_HEAD_DIM):
            psel[g * FOX_HEAD_DIM + dd, h * FOX_HEAD_DIM + dd] = 1.0
    b = lambda a: jnp.asarray(a, BF16)
    kk = np.arange(PAGE)
    c.update(triu=b(kk[:, None] > kk[None, :]), ones=b(np.ones((PAGE, PAGE), np.float32)))
    c.update(eq=b(eq), eqt=b(eq.T), cntq=jnp.asarray(cnt), ef=b(ef), eft=b(ef.T), ek=b(ek), ekt=b(ek.T),
             pkr=b(pkr), pq=b(pq), pk=b(pk), onesq=jnp.asarray(onesq), onesk=jnp.asarray(onesk), psel=b(psel))
    return c


def _pad_heads(w, nh, dh):
    k = w.shape[0]
    return jnp.pad(w.reshape(k, nh, dh), ((0, 0), (0, 0), (0, LANES - dh))).reshape(k, nh * LANES)


def _head_row(g, nh, scale=1.0):
    dh = g.shape[0]
    return jnp.tile(jnp.pad(g * scale, (0, LANES - dh)), nh)[None, :]


def _rope_tables(pos):
    half = MLA_ROPE // 2
    inv = ROPE_THETA ** (-jnp.arange(half, dtype=F32) / half)
    ang = pos.astype(F32)[:, None] * inv[None, :]
    cos, sin = jnp.cos(ang), jnp.sin(ang)
    t = pos.shape[0]
    z = lambda n: jnp.zeros((t, n), F32)
    cosq = jnp.concatenate([jnp.ones((t, MLA_NOPE), F32), cos, cos, z(LANES - MLA_NOPE - MLA_ROPE)], axis=1)
    sinq = jnp.concatenate([z(MLA_NOPE), -sin, sin, z(LANES - MLA_NOPE - MLA_ROPE)], axis=1)
    cosk = jnp.concatenate([cos, cos, z(LANES - MLA_ROPE)], axis=1)
    sink = jnp.concatenate([-sin, sin, z(LANES - MLA_ROPE)], axis=1)
    return jnp.stack([cosq, sinq, cosk, sink], axis=0)


def kernel(x_prompt, x_sample, cache_mla_latent, cache_mla_krope, cache_fox_k, cache_fox_v, cache_fox_logf,
           page_table, c_prompt, c_sample, w_ada, b_ada, g_attn, w_in, g_qa, w_qb, g_qn, g_qr, g_kva, g_kr,
           w_uk, w_uv, g_kn, g_fq, g_fk, b_f, w_br_a, w_br_b, w_o, g_ffn, w_rg, b_rg, w_re, b_re,
           w_e_gate, w_e_up, w_e_down):
    depth = w_ada.shape[0]
    assert depth == 1, "single-layer trunk"
    bsz, seq, d = x_prompt.shape
    dbsz, dseq, _ = x_sample.shape
    n_pool = cache_mla_latent.shape[1]
    past = page_table.shape[1] * PAGE
    n_p, n_s = bsz * seq, dbsz * dseq
    n_tok = n_p + n_s
    sc = _static_consts()
    half = MLA_ROPE // 2
    scale_mla = float((MLA_NOPE + MLA_ROPE) ** -0.5)
    scale_fox = float(FOX_HEAD_DIM ** -0.5)

    n_c = bsz + dbsz
    n_c_pad = -(-n_c // 8) * 8
    c_all = jnp.pad(jnp.concatenate([c_prompt, c_sample], axis=0), ((0, n_c_pad - n_c), (0, 0)))
    mod = _mod_call(c_all, w_ada[0], b_ada[0][None, :])
    mod3 = mod.reshape(n_c_pad, 1, N_MOD * d)
    mod_s = jnp.repeat(mod[bsz:bsz + dbsz], dseq, axis=0)

    def mod_spec_p(k, tm):
        return pl.BlockSpec((None, 1, d), lambda i: ((i * tm) // seq, 0, k))

    def mod_spec_s(k, tm):
        return pl.BlockSpec((tm, d), lambda i: (i, k))

    qa_w, kvl_w, kr_w, fq_w, fk_w, fv_w, fl_w, ga_w, gb_w = jnp.split(
        w_in[0], np.cumsum([MLA_Q_RANK, MLA_KV_RANK, MLA_ROPE, FOX_HEADS * FOX_HEAD_DIM,
                            FOX_KV_HEADS * FOX_HEAD_DIM, FOX_KV_HEADS * FOX_HEAD_DIM, FOX_HEADS, d]).tolist(), axis=1)
    kr_sw = jnp.concatenate([kr_w[:, half:], kr_w[:, :half]], axis=1)
    misc_w = jnp.concatenate([kr_w, kr_sw, fl_w, jnp.zeros((d, LANES - 2 * MLA_ROPE - FOX_HEADS), F32)], axis=1)
    w_small = jnp.concatenate([qa_w, kvl_w, _pad_heads(fq_w, FOX_HEADS, FOX_HEAD_DIM),
                               _pad_heads(fk_w, FOX_KV_HEADS, FOX_HEAD_DIM), fv_w, misc_w], axis=1).astype(BF16)
    w_gates = jnp.concatenate([ga_w, gb_w], axis=1).astype(BF16)
    dq = MLA_NOPE + MLA_ROPE
    wqb3 = w_qb[0].reshape(MLA_Q_RANK, MLA_HEADS, dq)
    wqb_p = jnp.pad(wqb3, ((0, 0), (0, 0), (0, LANES - dq))).reshape(MLA_Q_RANK, MLA_HEADS * LANES)
    wqb_sw3 = jnp.concatenate([jnp.zeros((MLA_Q_RANK, MLA_HEADS, MLA_NOPE), F32),
                               wqb3[:, :, MLA_NOPE + half:], wqb3[:, :, MLA_NOPE:MLA_NOPE + half],
                               jnp.zeros((MLA_Q_RANK, MLA_HEADS, LANES - dq), F32)], axis=2)
    wqb_cat = jnp.concatenate([wqb_p, wqb_sw3.reshape(MLA_Q_RANK, MLA_HEADS * LANES)], axis=1).astype(BF16)
    wuk2 = w_uk[0].reshape(MLA_KV_RANK, MLA_HEADS * MLA_NOPE)
    wuv2 = w_uv[0].reshape(MLA_KV_RANK, MLA_HEADS * MLA_V)
    wkv = _pad_heads(wuk2, MLA_HEADS, MLA_NOPE).astype(BF16)

    def rows_aug(w, nh, dh):
        return jnp.pad(w.T.reshape(nh, dh, w.shape[0]), ((0, 0), (0, DV_AUG - dh), (0, 0))).reshape(nh * DV_AUG, -1)

    def ones_aug(nh, dh):
        o = np.zeros((nh, DV_AUG, 1), np.float32)
        o[:, dh:, :] = 1.0
        return jnp.asarray(o.reshape(nh * DV_AUG, 1))

    wuvt = rows_aug(wuv2, MLA_HEADS, MLA_V).astype(BF16)
    wfvt = rows_aug(fv_w, FOX_KV_HEADS, FOX_HEAD_DIM).astype(BF16)

    gqr = g_qr[0]
    gq_a = jnp.concatenate([g_qn[0], gqr, jnp.zeros((LANES - dq,), F32)])
    gq_b = jnp.concatenate([jnp.zeros((MLA_NOPE,), F32), gqr[half:], gqr[:half], jnp.zeros((LANES - dq,), F32)])
    gq = jnp.stack([jnp.tile(gq_a, MLA_HEADS), jnp.tile(gq_b, MLA_HEADS)], axis=0) * scale_mla
    gkr0 = g_kr[0]
    gkr = jnp.stack([jnp.pad(gkr0, (0, LANES - MLA_ROPE)),
                     jnp.pad(jnp.concatenate([gkr0[half:], gkr0[:half]]), (0, LANES - MLA_ROPE))], axis=0)
    bf_row = jnp.zeros((1, LANES), F32).at[0, LF0:LF0 + FOX_HEADS].set(b_f[0])
    proj_a = [g_attn, w_small, wqb_cat, wkv, g_qa, g_kva, gq, gkr,
              _head_row(g_fq[0], FOX_HEADS, scale_fox), _head_row(g_fk[0], FOX_KV_HEADS),
              _head_row(g_kn[0], MLA_HEADS), bf_row]
    proj_b = [sc["eq"], sc["eqt"], sc["cntq"], sc["ef"], sc["eft"], sc["ek"], sc["ekt"],
              sc["pkr"], sc["pq"], sc["pk"], sc["onesq"], sc["onesk"],
              wuvt, ones_aug(MLA_HEADS, MLA_V), wfvt, ones_aug(FOX_KV_HEADS, FOX_HEAD_DIM)]
    w_r = jnp.concatenate([w_re[0], w_rg[0], jnp.zeros((d, LANES - N_EXPERTS - N_GROUPS), F32)], axis=1)
    w_r_hi = w_r.astype(BF16)
    w_r_lo = (w_r - w_r_hi.astype(F32)).astype(BF16)
    b_r = jnp.concatenate([b_re[0], b_rg[0], jnp.zeros((LANES - N_EXPERTS - N_GROUPS,), F32)])[None, :]
    consts = {"proj_a": proj_a, "proj_b": proj_b,
              "merge": [g_ffn, w_br_a[0].astype(BF16), w_br_b[0].astype(BF16), w_o[0].astype(BF16),
                        w_r_hi, w_r_lo, b_r]}

    tm_p = min(TM_PROJ, n_p)
    tm_s = min(TM_PROJ, n_s)
    tabs_p = _rope_tables(jnp.arange(seq, dtype=jnp.int32))
    tabs_s = _rope_tables(past + (jnp.arange(tm_s, dtype=jnp.int32) % dseq))
    r_p = np.arange(tm_p)
    tri_p = jnp.asarray(r_p[None, :] <= r_p[:, None], BF16)
    r_s = np.arange(tm_s)
    tri_s = jnp.asarray((r_s[None, :] <= r_s[:, None]) & (r_s[None, :] // dseq == r_s[:, None] // dseq), BF16)
    xp2 = x_prompt.reshape(n_p, d)
    xs2 = x_sample.reshape(n_s, d)
    (hb_p, qmla_p, lat_p, kr_p, qfa_p, kfp_p, kfa_p, vf_p, lf_p, cum_p, kmla_p, vmt_p, vft_p) = _proj_call(
        xp2, mod3, mod3, (mod_spec_p(0, tm_p), mod_spec_p(1, tm_p)), tabs_p, tri_p, consts,
        seq_rows=seq, with_kv=True)
    (hb_s, qmla_s, lat_s, kr_s, qfa_s, kfp_s, kfa_s, vf_s, lf_s, cum_s, _, _, _) = _proj_call(
        xs2, mod_s, mod_s, (mod_spec_s(0, tm_s), mod_spec_s(1, tm_s)), tabs_s, tri_s, consts,
        seq_rows=tm_s, with_kv=False)

    sg_p = _gates_call(hb_p, w_gates)
    sg_s = _gates_call(hb_s, w_gates)

    o_mla_p = _attn_call(qmla_p.reshape(bsz, seq, -1), kmla_p.reshape(bsz, seq, -1), vmt_p,
                         nkv=MLA_HEADS, rep=1, dv=MLA_V, name="attn_mla")
    o_fox_p = _attn_call(qfa_p.reshape(bsz, seq, -1), kfa_p.reshape(bsz, seq, -1), vft_p,
                         nkv=FOX_KV_HEADS, rep=FOX_HEADS // FOX_KV_HEADS, dv=FOX_HEAD_DIM, name="attn_fox")

    nrow = dseq * MLA_HEADS
    q4 = qmla_s.reshape(dbsz, dseq, MLA_HEADS, LANES)
    eye_h = jnp.eye(MLA_HEADS, dtype=BF16)
    qn_bd = (q4[:, :, :, None, :MLA_NOPE] * eye_h[None, None, :, :, None]).reshape(dbsz, nrow, MLA_HEADS * MLA_NOPE)
    qr = q4[..., MLA_NOPE:dq].reshape(dbsz, nrow, MLA_ROPE)
    wukt = wuk2.T
    gcol = jnp.tile(g_kn[0], MLA_HEADS)[:, None]
    mla_in = [qn_bd, qr, lat_s.reshape(dbsz, dseq, -1), kr_s[:, :MLA_ROPE].reshape(dbsz, dseq, MLA_ROPE),
              wukt, gcol, wuv2.astype(BF16), cache_mla_latent[0], cache_mla_krope[0].transpose(0, 2, 1)]

    qf4 = qfa_s.reshape(dbsz, dseq, FOX_HEADS, LANES)[..., :FOX_HEAD_DIM]
    rep = FOX_HEADS // FOX_KV_HEADS
    onehot_g = jnp.asarray(np.arange(FOX_HEADS)[:, None] // rep == np.arange(FOX_KV_HEADS)[None, :], BF16)
    qbd = (qf4[:, :, :, None, :] * onehot_g[None, None, :, :, None]).reshape(dbsz, nrow, FOX_KV_HEADS * FOX_HEAD_DIM)
    kf_s = kfp_s.reshape(n_s, FOX_KV_HEADS, LANES)[:, :, :FOX_HEAD_DIM]
    cn = cum_s[:, LF0:LF0 + FOX_HEADS].reshape(dbsz, dseq, FOX_HEADS)
    cn_col = cn.reshape(dbsz, nrow, 1)
    cn_t = jnp.pad(cn.transpose(0, 2, 1), ((0, 0), (0, 0), (0, LANES - dseq)))
    kvw = FOX_KV_HEADS * FOX_HEAD_DIM
    clft = cache_fox_logf[0].transpose(0, 2, 1)
    cet = _suffix_call(clft, sc["triu"], sc["ones"])
    ckt = cache_fox_k[0].transpose(0, 2, 3, 1).reshape(n_pool, kvw, PAGE)
    cvt = cache_fox_v[0].transpose(0, 2, 3, 1).reshape(n_pool, kvw, PAGE)
    fox_in = [qbd, kf_s.reshape(dbsz, dseq, kvw), vf_s.reshape(dbsz, dseq, kvw), cn_col, cn_t, sc["psel"],
              ckt, cvt, cet]
    o_mla_s, o_fox_s = _paged_call(page_table, mla_in, fox_in)
    o_mla_s = o_mla_s.reshape(n_s, -1)
    o_fox_s = o_fox_s.reshape(n_s, -1)

    x1_p, h2_p, rt_p = _merge_call(xp2, o_mla_p, o_fox_p, sg_p, mod3, mod3, mod3,
                                   (mod_spec_p(2, tm_p), mod_spec_p(3, tm_p), mod_spec_p(4, tm_p)), consts,
                                   o_transposed=True)
    x1_s, h2_s, rt_s = _merge_call(xs2, o_mla_s, o_fox_s, sg_s, mod_s, mod_s, mod_s,
                                   (mod_spec_s(2, tm_s), mod_spec_s(3, tm_s), mod_spec_s(4, tm_s)), consts,
                                   o_transposed=False)

    h2 = jnp.concatenate([h2_p, h2_s], axis=0)
    rt = jnp.concatenate([rt_p, rt_s], axis=0)
    e_flat = rt[:, 0:TOP_K].astype(jnp.int32).reshape(-1)
    n_pairs = n_tok * TOP_K
    onehot = (e_flat[:, None] == jnp.arange(N_EXPERTS, dtype=jnp.int32)[None, :]).astype(jnp.int32)
    incl = jnp.cumsum(onehot, axis=0)
    counts = incl[-1]
    rank = jnp.sum((incl - onehot) * onehot, axis=1)
    tiles_e = (counts + TG_MOE - 1) // TG_MOE
    tile_end = jnp.cumsum(tiles_e)
    tile_start = tile_end - tiles_e
    dest = jnp.sum(onehot * tile_start[None, :], axis=1) * TG_MOE + rank
    n_tiles = n_pairs // TG_MOE + N_EXPERTS
    p_total = n_tiles * TG_MOE
    src_tok = jnp.zeros((p_total,), jnp.int32).at[dest].set(jnp.arange(n_pairs, dtype=jnp.int32) // TOP_K)
    n_valid = tile_end[-1:].astype(jnp.int32)
    tile_ids = jnp.arange(n_tiles, dtype=jnp.int32)
    tile_e = jnp.sum((jnp.minimum(tile_ids, n_valid[0] - 1)[:, None] >= tile_end[None, :]).astype(jnp.int32), axis=1)
    tile_e = jnp.minimum(tile_e, N_EXPERTS - 1).astype(jnp.int32)
    ys = _gmm_call(tile_e, n_valid, src_tok.reshape(n_tiles, 1, TG_MOE), h2, w_e_gate[0], w_e_up[0], w_e_down[0])
    dest2 = dest.reshape(n_tok, TOP_K)
    pick = lambda lo, hi, slot: jnp.take(ys, dest2[lo:hi, slot], axis=0, mode="clip")

    y_p = _final_call(x1_p, pick(0, n_p, 0), pick(0, n_p, 1), rt_p, mod3,
                      lambda tm: pl.BlockSpec((None, 1, d), lambda i: ((i * tm) // seq, 0, 5)))
    y_s = _final_call(x1_s, pick(n_p, n_tok, 0), pick(n_p, n_tok, 1), rt_s, mod_s,
                      lambda tm: pl.BlockSpec((tm, d), lambda i: (i, 5)))

    def leaves(lat, kr, kfp, vf, lf, bb, tt):
        return (lat.reshape(1, bb, tt, MLA_KV_RANK),
                kr[:, :MLA_ROPE].reshape(1, bb, tt, MLA_ROPE),
                kfp.reshape(-1, FOX_KV_HEADS, LANES)[:, :, :FOX_HEAD_DIM].reshape(1, bb, tt, FOX_KV_HEADS, FOX_HEAD_DIM),
                vf.reshape(1, bb, tt, FOX_KV_HEADS, FOX_HEAD_DIM),
                lf[:, LF0:LF0 + FOX_HEADS].reshape(1, bb, tt, FOX_HEADS))

    return ((y_p.reshape(bsz, seq, d), y_s.reshape(dbsz, dseq, d))
            + leaves(lat_p, kr_p, kfp_p, vf_p, lf_p, bsz, seq)
            + leaves(lat_s, kr_s, kfp_s, vf_s, lf_s, dbsz, dseq))
```

```python
import functools

import numpy as np
import jax
import jax.numpy as jnp
from jax import lax
from jax.experimental import pallas as pl
from jax.experimental.pallas import tpu as pltpu

F32 = jnp.float32
BF16 = jnp.bfloat16

MLA_HEADS, MLA_NOPE, MLA_ROPE, MLA_V = 8, 64, 32, 64
MLA_Q_RANK, MLA_KV_RANK = 512, 256
FOX_HEADS, FOX_KV_HEADS, FOX_HEAD_DIM = 8, 4, 64
N_GROUPS, EXPERTS_PER_GROUP, TOP_K = 4, 8, 2
N_EXPERTS = N_GROUPS * EXPERTS_PER_GROUP
N_MOD = 6
ROPE_THETA = 10000.0
RMS_EPS = 1e-6
PAGE = 128
LANES = 128
AUG0 = 64
LF0 = 64
DV_AUG = 80

VMEM_LIMIT = 56 * 1024 * 1024
TM_PROJ = 256
TM_GATES = 512
TN_GATES = 1024
TN_MOD = 1024
TQ_ATTN = 1024
PAGES_PER_CHUNK = 32
MLA_SUB_KEYS = 1024
PREPASS_PAGES = 256
TG_MOE = 256
GMM_SLOTS = 3
TM_FINAL = 512


def _cparams(sem):
    return pltpu.CompilerParams(dimension_semantics=sem, vmem_limit_bytes=VMEM_LIMIT)


def _vmem_whole():
    return pl.BlockSpec(memory_space=pltpu.VMEM)


def _dot(a, b):
    return jnp.dot(a, b, preferred_element_type=F32)


def _dot_nt(a, b):
    return lax.dot_general(a, b, (((1,), (1,)), ((), ())), preferred_element_type=F32)


def _split3(x):
    hi = x.astype(BF16)
    r1 = x - hi.astype(F32)
    mid = r1.astype(BF16)
    lo = (r1 - mid.astype(F32)).astype(BF16)
    return hi, mid, lo


def _group_rsqrt(v, e_ref, inv_cnt):
    ss = _dot((v * v).astype(BF16), e_ref[...])
    return lax.rsqrt(ss * inv_cnt + RMS_EPS)


def _group_bcast(r, et2_ref):
    rh = r.astype(BF16)
    rl = (r - rh.astype(F32)).astype(BF16)
    return _dot(jnp.concatenate([rh, rl], axis=1), et2_ref[...])


def _mod_kernel(c_ref, w_ref, b_ref, o_ref):
    c = c_ref[...]
    a = (c / (1.0 + jnp.exp(-c))).astype(BF16)
    o_ref[...] = _dot(a, w_ref[...].astype(BF16)) + b_ref[...]


def _mod_call(c_all, w_ada, b_ada):
    rows, d = c_all.shape
    n = w_ada.shape[1]
    return pl.pallas_call(
        _mod_kernel,
        out_shape=jax.ShapeDtypeStruct((rows, n), F32),
        grid=(n // TN_MOD,),
        in_specs=[pl.BlockSpec((rows, d), lambda j: (0, 0)),
                  pl.BlockSpec((d, TN_MOD), lambda j: (0, j)),
                  pl.BlockSpec((1, TN_MOD), lambda j: (0, j))],
        out_specs=pl.BlockSpec((rows, TN_MOD), lambda j: (0, j)),
        compiler_params=_cparams(("arbitrary",)),
        name="mod",
    )(c_all, w_ada, b_ada)


C_QA = (0, 512)
C_KVL = (512, 768)
C_FQ = (768, 1792)
C_FK = (1792, 2304)
C_FV = (2304, 2560)
C_MISC = (2560, 2688)


def _proj_kernel(x_ref, m0_ref, m1_ref, gattn_ref, wsm_ref, wqb_ref, wkv_ref,
                 gqa_ref, gkva_ref, gq_ref, gkr_ref, gfq_ref, gfk_ref, gkn_ref, bf_ref,
                 tab_ref, eq_ref, eqt_ref, cntq_ref, ef_ref, eft_ref, ek_ref, ekt_ref,
                 pkr_ref, pq_ref, pk_ref, onesq_ref, onesk_ref, wuvt_ref, onesm_ref, wfvt_ref, onesf_ref,
                 tri_ref,
                 hb_ref, qmla_ref, lat_ref, kr_ref, qfa_ref, kfp_ref, kfa_ref, vf_ref,
                 lf_ref, cum_ref, kmla_ref, vmt_ref, vft_ref, carry_ref, *, tiles_per_seq, with_kv):
    i = pl.program_id(0)
    x = x_ref[...]
    xn = x * lax.rsqrt(jnp.mean(x * x, axis=-1, keepdims=True) + RMS_EPS) * gattn_ref[...]
    h = xn * (1.0 + m1_ref[...]) + m0_ref[...]
    hb = h.astype(BF16)
    hb_ref[...] = hb
    z = _dot(hb, wsm_ref[...])
    qa = z[:, C_QA[0]:C_QA[1]]
    kvl = z[:, C_KVL[0]:C_KVL[1]]
    fq = z[:, C_FQ[0]:C_FQ[1]]
    fk = z[:, C_FK[0]:C_FK[1]]
    fv = z[:, C_FV[0]:C_FV[1]]
    misc = z[:, C_MISC[0]:C_MISC[1]]
    cosq, sinq, cosk, sink = tab_ref[0], tab_ref[1], tab_ref[2], tab_ref[3]

    qan = qa * lax.rsqrt(jnp.mean(qa * qa, axis=-1, keepdims=True) + RMS_EPS) * gqa_ref[...]
    qq = _dot(qan.astype(BF16), wqb_ref[...])
    nq = MLA_HEADS * LANES
    q, qsw = qq[:, :nq], qq[:, nq:]
    sc = _group_bcast(_group_rsqrt(q, eq_ref, cntq_ref[...]), eqt_ref)
    gq = gq_ref[...]
    for hh in range(MLA_HEADS):
        sl = slice(hh * LANES, (hh + 1) * LANES)
        qo = sc[:, sl] * (q[:, sl] * (gq[0:1, sl] * cosq) + qsw[:, sl] * (gq[1:2, sl] * sinq))
        qmla_ref[:, sl] = qo.astype(BF16)

    lat = kvl * lax.rsqrt(jnp.mean(kvl * kvl, axis=-1, keepdims=True) + RMS_EPS) * gkva_ref[...]
    lat_ref[...] = lat

    lane = lax.broadcasted_iota(jnp.int32, misc.shape, 1)
    kr2 = jnp.where(lane < MLA_ROPE, misc * misc, 0.0)
    rk = lax.rsqrt(jnp.sum(kr2, axis=-1, keepdims=True) * (1.0 / MLA_ROPE) + RMS_EPS)
    msw = pltpu.roll(misc, LANES - MLA_ROPE, 1)
    gkr = gkr_ref[...]
    kr = rk * (misc * (gkr[0:1] * cosk) + msw * (gkr[1:2] * sink))
    kr_ref[...] = kr

    xl = misc + bf_ref[...]
    lf = jnp.minimum(xl, 0.0) - jnp.log1p(jnp.exp(-jnp.abs(xl)))
    lf = jnp.where((lane >= LF0) & (lane < LF0 + FOX_HEADS), lf, 0.0)
    lf_ref[...] = lf
    l_hi, l_mid, l_lo = _split3(lf)
    tri = tri_ref[...]
    cum = _dot(tri, l_hi) + _dot(tri, l_mid) + _dot(tri, l_lo)
    if tiles_per_seq > 1:
        @pl.when(i % tiles_per_seq == 0)
        def _():
            carry_ref[...] = jnp.zeros_like(carry_ref)
        cum = cum + carry_ref[...]
        carry_ref[...] = cum[cum.shape[0] - 1:, :]
    cum_ref[...] = cum
    c_hi, c_mid, c_lo = _split3(cum)
    cs = jnp.concatenate([c_hi, c_mid, c_lo], axis=1)
    augq = _dot(cs, pq_ref[...]) + onesq_ref[...]
    augk = _dot(cs, pk_ref[...]) + onesk_ref[...]

    scf = _group_bcast(_group_rsqrt(fq, ef_ref, 1.0 / FOX_HEAD_DIM), eft_ref)
    qfa_ref[...] = (fq * scf * gfq_ref[...] + augq).astype(BF16)
    sck = _group_bcast(_group_rsqrt(fk, ek_ref, 1.0 / FOX_HEAD_DIM), ekt_ref)
    kf = fk * sck * gfk_ref[...]
    kfp_ref[...] = kf
    kfa_ref[...] = (kf + augk).astype(BF16)
    vf_ref[...] = fv

    if with_kv:
        lat_b = lat.astype(BF16)
        kn = _dot(lat_b, wkv_ref[...])
        scn = _group_bcast(_group_rsqrt(kn, ef_ref, 1.0 / MLA_NOPE), eft_ref)
        kp = _dot(kr.astype(BF16), pkr_ref[...])
        kmla_ref[...] = (kn * scn * gkn_ref[...] + kp).astype(BF16)
        vmt_ref[...] = (_dot_nt(wuvt_ref[...], lat_b) + onesm_ref[...]).astype(BF16)
        vft_ref[...] = (_dot_nt(wfvt_ref[...], hb) + onesf_ref[...]).astype(BF16)
    else:
        kmla_ref[...] = jnp.zeros_like(kmla_ref)
        vmt_ref[...] = jnp.zeros_like(vmt_ref)
        vft_ref[...] = jnp.zeros_like(vft_ref)


def _proj_call(x2d, m0, m1, mod_specs, tabs, tri, consts, *, seq_rows, with_kv):
    n, d = x2d.shape
    tm = min(TM_PROJ, n)
    tiles_per_seq = max(seq_rows // tm, 1)
    n_tab_tiles = tabs.shape[1] // tm
    row = lambda c: pl.BlockSpec((tm, c), lambda i: (i, 0))
    in_specs = [row(d), mod_specs[0], mod_specs[1]] + [_vmem_whole()] * 12
    in_specs += [pl.BlockSpec((4, tm, LANES), lambda i: (0, i % n_tab_tiles, 0))]
    in_specs += [_vmem_whole()] * 16
    in_specs += [pl.BlockSpec((tm, tm), lambda i: (0, 0))]
    nq = MLA_HEADS * LANES
    out_cols = [(d, BF16), (nq, BF16), (MLA_KV_RANK, F32), (LANES, F32), (FOX_HEADS * LANES, BF16),
                (FOX_KV_HEADS * LANES, F32), (FOX_KV_HEADS * LANES, BF16),
                (FOX_KV_HEADS * FOX_HEAD_DIM, F32),
                (LANES, F32), (LANES, F32), (nq, BF16)]
    out_rows_t = [MLA_HEADS * DV_AUG, FOX_KV_HEADS * DV_AUG]
    col = lambda r: pl.BlockSpec((r, tm), lambda i: (0, i))
    return pl.pallas_call(
        functools.partial(_proj_kernel, tiles_per_seq=tiles_per_seq, with_kv=with_kv),
        out_shape=([jax.ShapeDtypeStruct((n, c), dt) for c, dt in out_cols]
                   + [jax.ShapeDtypeStruct((r, n), BF16) for r in out_rows_t]),
        grid=(n // tm,),
        in_specs=in_specs,
        out_specs=[row(c) for c, _ in out_cols] + [col(r) for r in out_rows_t],
        scratch_shapes=[pltpu.VMEM((1, LANES), F32)],
        compiler_params=_cparams(("arbitrary",)),
        name="proj",
    )(x2d, m0, m1, *consts["proj_a"], tabs, *consts["proj_b"], tri)


def _gates_kernel(h_ref, w_ref, o_ref):
    z = _dot(h_ref[...], w_ref[...])
    o_ref[...] = (1.0 / (1.0 + jnp.exp(-z))).astype(BF16)


def _gates_call(hb, wg):
    n, d = hb.shape
    nc = wg.shape[1]
    tm = min(TM_GATES, n)
    return pl.pallas_call(
        _gates_kernel,
        out_shape=jax.ShapeDtypeStruct((n, nc), BF16),
        grid=(nc // TN_GATES, n // tm),
        in_specs=[pl.BlockSpec((tm, d), lambda j, i: (i, 0)),
                  pl.BlockSpec((d, TN_GATES), lambda j, i: (0, j))],
        out_specs=pl.BlockSpec((tm, TN_GATES), lambda j, i: (i, j)),
        compiler_params=_cparams(("arbitrary", "arbitrary")),
        name="gates",
    )(hb, wg)


def _attn_kernel(ii_ref, jj_ref, q_ref, k_ref, vt_ref, o_ref, m_ref, acc_ref, *, nkv, rep, dv):
    i = ii_ref[pl.program_id(1)]
    j = jj_ref[pl.program_id(1)]
    tq = q_ref.shape[1]

    @pl.when(j == 0)
    def _():
        m_ref[...] = jnp.full_like(m_ref, -jnp.inf)
        acc_ref[...] = jnp.zeros_like(acc_ref)

    def update(masked):
        for g in range(nkv):
            k = k_ref[0, :, g * LANES:(g + 1) * LANES]
            qs = [q_ref[0, :, (g * rep + e) * LANES:(g * rep + e + 1) * LANES] for e in range(rep)]
            q = qs[0] if rep == 1 else jnp.concatenate(qs, axis=0)
            s = _dot_nt(k, q)
            if masked:
                rows = lax.broadcasted_iota(jnp.int32, s.shape, 0)
                cols = lax.broadcasted_iota(jnp.int32, s.shape, 1)
                s = jnp.where(rows <= cols % tq, s, -jnp.inf)
            m_prev = m_ref[g]
            m_new = jnp.maximum(m_prev, jnp.max(s, axis=0, keepdims=True))
            alpha = jnp.exp(m_prev - m_new)
            p = jnp.exp(s - m_new).astype(BF16)
            acc_ref[g] = alpha * acc_ref[g] + _dot(vt_ref[g * DV_AUG:(g + 1) * DV_AUG, :], p)
            m_ref[g] = m_new

    @pl.when(j < i)
    def _():
        update(False)

    @pl.when(j == i)
    def _():
        update(True)
        for g in range(nkv):
            for e in range(rep):
                a = acc_ref[g][:, e * tq:(e + 1) * tq]
                h = g * rep + e
                o_ref[h * dv:(h + 1) * dv, :] = (a[0:dv] / a[dv:dv + 1]).astype(BF16)


def _attn_call(q, k, vt, *, nkv, rep, dv, name):
    b, t, _ = q.shape
    tq = min(TQ_ATTN, t)
    nq = t // tq
    nh = nkv * rep
    pairs = [(i, j) for i in range(nq) for j in range(i + 1)]
    ii = jnp.asarray([p[0] for p in pairs], jnp.int32)
    jj = jnp.asarray([p[1] for p in pairs], jnp.int32)
    grid_spec = pltpu.PrefetchScalarGridSpec(
        num_scalar_prefetch=2, grid=(b, len(pairs)),
        in_specs=[pl.BlockSpec((1, tq, q.shape[2]), lambda bb, p, ii_, jj_: (bb, ii_[p], 0)),
                  pl.BlockSpec((1, tq, k.shape[2]), lambda bb, p, ii_, jj_: (bb, jj_[p], 0)),
                  pl.BlockSpec((nkv * DV_AUG, tq), lambda bb, p, ii_, jj_: (0, bb * nq + jj_[p]))],
        out_specs=pl.BlockSpec((nh * dv, tq), lambda bb, p, ii_, jj_: (0, bb * nq + ii_[p])),
        scratch_shapes=[pltpu.VMEM((nkv, 1, rep * tq), F32), pltpu.VMEM((nkv, DV_AUG, rep * tq), F32)])
    return pl.pallas_call(
        functools.partial(_attn_kernel, nkv=nkv, rep=rep, dv=dv),
        out_shape=jax.ShapeDtypeStruct((nh * dv, b * t), BF16),
        grid_spec=grid_spec,
        compiler_params=_cparams(("arbitrary", "arbitrary")),
        name=name,
    )(ii, jj, q, k, vt)


def _softmax_update(s, pv, m_ref, l_ref, acc_ref):
    m_prev = m_ref[...]
    m_new = jnp.maximum(m_prev, jnp.max(s, axis=-1, keepdims=True))
    alpha = jnp.exp(m_prev - m_new)
    p = jnp.exp(s - m_new)
    l_ref[...] = alpha * l_ref[...] + jnp.sum(p, axis=-1, keepdims=True)
    acc_ref[...] = alpha * acc_ref[...] + pv(p.astype(BF16))
    m_ref[...] = m_new


def _chunk_loop(b, n_seq, n_chunks, start, wait, compute):
    @pl.when(b == 0)
    def _():
        start(b, 0, 0)

    def pair(c2, carry):
        c = 2 * c2
        start(b, c + 1, 1)
        wait(0)
        compute(0)

        @pl.when(c + 2 < n_chunks)
        def _():
            start(b, c + 2, 0)

        @pl.when(jnp.logical_and(c + 2 >= n_chunks, b + 1 < n_seq))
        def _():
            start(b + 1, 0, 0)

        wait(1)
        compute(1)
        return carry

    lax.fori_loop(0, n_chunks // 2, pair, 0, unroll=True)


def _head_select(full, dv):
    rows = lax.broadcasted_iota(jnp.int32, full.shape, 0)
    cols = lax.broadcasted_iota(jnp.int32, full.shape, 1)
    nh = full.shape[1] // dv
    sel = jnp.where(cols // dv == rows % nh, full, 0.0)
    return jnp.sum(sel.reshape(full.shape[0] // nh, nh, full.shape[1]), axis=1)


class _Branch:
    def __init__(self, copies, init_once, init_seq, compute, finish):
        self.copies, self.init_once, self.init_seq, self.compute, self.finish = (
            copies, init_once, init_seq, compute, finish)


def _mla_branch(pt_ref, qn_ref, qr_ref, latn_ref, krn_ref, wukt_ref, gcol_ref, wuv_ref,
                clat_ref, ckr_ref, o_ref,
                a_ref, wg_ref, latbuf, krbuf, newlat, newkr, sem, m_ref, l_ref, acc_ref,
                *, n_chunks, pages):
    ck = pages * PAGE
    sub = min(MLA_SUB_KEYS, ck)
    nrow = qn_ref.shape[1]
    nup = wukt_ref.shape[0]
    t_new = latn_ref.shape[1]

    def copies(seq, k, slot):
        out = []
        for p in range(pages):
            pid = pt_ref[seq, k * pages + p]
            out.append(pltpu.make_async_copy(clat_ref.at[pid], latbuf.at[slot, pl.ds(p * PAGE, PAGE)],
                                             sem.at[0, slot]))
            out.append(pltpu.make_async_copy(ckr_ref.at[pid], krbuf.at[slot, :, pl.ds(p * PAGE, PAGE)],
                                             sem.at[1, slot]))
        return out

    def init_once():
        w = wukt_ref[...]
        a_ref[0:nup, :] = w.astype(BF16)
        wg_ref[...] = (w * gcol_ref[...]).astype(BF16)
        newlat[...] = jnp.zeros_like(newlat)
        newkr[...] = jnp.zeros_like(newkr)

    def init_seq():
        a_ref[nup:nup + nrow, :] = _dot(qn_ref[0], wg_ref[...]).astype(BF16)
        m_ref[...] = jnp.full_like(m_ref, -jnp.inf)
        l_ref[...] = jnp.zeros_like(l_ref)
        acc_ref[...] = jnp.zeros_like(acc_ref)

    def nope_scores(lat_b):
        r_all = _dot_nt(a_ref[...], lat_b)
        kup = r_all[0:nup]
        nk = kup.shape[1]
        ss = jnp.sum((kup * kup).reshape(MLA_HEADS, MLA_NOPE, nk), axis=1)
        rn = lax.rsqrt(ss * (1.0 / MLA_NOPE) + RMS_EPS)
        sn = r_all[nup:nup + nrow].reshape(nrow // MLA_HEADS, MLA_HEADS, nk) * rn[None]
        return sn.reshape(nrow, nk)

    def compute(slot):
        lat_b = latbuf[slot].astype(BF16)
        parts = [nope_scores(lat_b[j * sub:(j + 1) * sub]) for j in range(ck // sub)]
        s = jnp.concatenate(parts, axis=1) + _dot(qr_ref[0], krbuf[slot].astype(BF16))
        _softmax_update(s, lambda p: _dot(p, lat_b), m_ref, l_ref, acc_ref)

    def finish():
        newlat[0:t_new, :] = latn_ref[0]
        newkr[0:t_new, :] = krn_ref[0]
        lat_n = newlat[...].astype(BF16)
        s = nope_scores(lat_n) + _dot_nt(qr_ref[0], newkr[...].astype(BF16))
        rows = lax.broadcasted_iota(jnp.int32, s.shape, 0)
        cols = lax.broadcasted_iota(jnp.int32, s.shape, 1)
        s = jnp.where(cols <= rows // MLA_HEADS, s, -jnp.inf)
        _softmax_update(s, lambda p: _dot(p, lat_n), m_ref, l_ref, acc_ref)
        o_lat = (acc_ref[...] / l_ref[...]).astype(BF16)
        o_ref[0] = _head_select(_dot(o_lat, wuv_ref[...]), MLA_V).astype(BF16)

    return _Branch(copies, init_once, init_seq, compute, finish)


def _mla_scratch(ck, nrow, nup):
    return [pltpu.VMEM((nup + nrow, MLA_KV_RANK), BF16),
            pltpu.VMEM((nup, MLA_KV_RANK), BF16),
            pltpu.VMEM((2, ck, MLA_KV_RANK), F32),
            pltpu.VMEM((2, MLA_ROPE, ck), F32),
            pltpu.VMEM((LANES, MLA_KV_RANK), F32),
            pltpu.VMEM((LANES, MLA_ROPE), F32),
            pltpu.SemaphoreType.DMA((2, 2)),
            pltpu.VMEM((nrow, 1), F32), pltpu.VMEM((nrow, 1), F32),
            pltpu.VMEM((nrow, MLA_KV_RANK), F32)]


def _suffix_kernel(x_ref, triu_ref, ones_ref, o_ref):
    npg, nh, pg = x_ref.shape
    x = x_ref[...].reshape(npg * nh, pg)
    e = jnp.zeros(x.shape, F32)
    t = jnp.zeros(x.shape, F32)
    for piece in _split3(x):
        e = e + _dot(piece, triu_ref[...])
        t = t + _dot(piece, ones_ref[...])
    o_ref[:, 0:nh, :] = e.reshape(npg, nh, pg)
    o_ref[:, nh:2 * nh, :] = t.reshape(npg, nh, pg)


def _suffix_call(lft, triu, ones):
    n_pool, nh, pg = lft.shape
    npg = min(PREPASS_PAGES, n_pool)
    assert n_pool % npg == 0
    return pl.pallas_call(
        _suffix_kernel,
        out_shape=jax.ShapeDtypeStruct((n_pool, 2 * nh, pg), F32),
        grid=(n_pool // npg,),
        in_specs=[pl.BlockSpec((npg, nh, pg), lambda i: (i, 0, 0)), _vmem_whole(), _vmem_whole()],
        out_specs=pl.BlockSpec((npg, 2 * nh, pg), lambda i: (i, 0, 0)),
        compiler_params=_cparams(("arbitrary",)),
        name="suffix",
    )(lft, triu, ones)


def _fox_branch(pt_ref, qbd_ref, kn_ref, vn_ref, cncol_ref, cnt_ref, psel_ref,
                ck_ref, cv_ref, cet_ref, o_ref,
                kbuf, vbuf, etbuf, newk, newv, sem, m_ref, l_ref, acc_ref, sufc_ref,
                *, n_chunks, pages):
    ck = pages * PAGE
    nrow = qbd_ref.shape[1]
    t_new = kn_ref.shape[1]
    nt = nrow // FOX_HEADS

    def copies(seq, k, slot):
        out = []
        kk = n_chunks - 1 - k
        for p in range(pages):
            pid = pt_ref[seq, kk * pages + p]
            dst = pl.ds(p * PAGE, PAGE)
            out.append(pltpu.make_async_copy(ck_ref.at[pid], kbuf.at[slot, :, dst], sem.at[0, slot]))
            out.append(pltpu.make_async_copy(cv_ref.at[pid], vbuf.at[slot, :, dst], sem.at[1, slot]))
            out.append(pltpu.make_async_copy(cet_ref.at[pid], etbuf.at[slot, :, dst], sem.at[2, slot]))
        return out

    def init_once():
        newk[...] = jnp.zeros_like(newk)
        newv[...] = jnp.zeros_like(newv)

    def init_seq():
        m_ref[...] = jnp.full_like(m_ref, -jnp.inf)
        l_ref[...] = jnp.zeros_like(l_ref)
        acc_ref[...] = jnp.zeros_like(acc_ref)
        sufc_ref[...] = jnp.zeros_like(sufc_ref)

    def compute(slot):
        cncol = cncol_ref[0]
        s = _dot(qbd_ref[0], kbuf[slot].astype(BF16))
        carry = sufc_ref[...]
        pieces = [None] * pages
        for p in reversed(range(pages)):
            lanes_p = slice(p * PAGE, (p + 1) * PAGE)
            pieces[p] = etbuf[slot, 0:FOX_HEADS, lanes_p] + carry
            carry = carry + etbuf[slot, FOX_HEADS:2 * FOX_HEADS, lanes_p]
        sufc_ref[...] = carry
        bias = jnp.concatenate(pieces, axis=1)
        s = (s.reshape(nt, FOX_HEADS, ck) + bias[None]).reshape(nrow, ck) + cncol
        v_b = vbuf[slot].astype(BF16)
        _softmax_update(s, lambda p: _dot_nt(p, v_b), m_ref, l_ref, acc_ref)

    def finish():
        newk[0:t_new, :] = kn_ref[0]
        newv[0:t_new, :] = vn_ref[0]
        s = _dot_nt(qbd_ref[0], newk[...].astype(BF16))
        s = (s.reshape(nt, FOX_HEADS, LANES) - cnt_ref[0][None]).reshape(nrow, LANES) + cncol_ref[0]
        rows = lax.broadcasted_iota(jnp.int32, s.shape, 0)
        cols = lax.broadcasted_iota(jnp.int32, s.shape, 1)
        s = jnp.where(cols <= rows // FOX_HEADS, s, -jnp.inf)
        v_n = newv[...].astype(BF16)
        _softmax_update(s, lambda p: _dot(p, v_n), m_ref, l_ref, acc_ref)
        o = (acc_ref[...] / l_ref[...]).astype(BF16)
        o_ref[0] = _head_select(_dot(o, psel_ref[...]), FOX_HEAD_DIM).astype(BF16)

    return _Branch(copies, init_once, init_seq, compute, finish)


def _fox_scratch(ck, nrow):
    kvw = FOX_KV_HEADS * FOX_HEAD_DIM
    return [pltpu.VMEM((2, kvw, ck), F32), pltpu.VMEM((2, kvw, ck), F32),
            pltpu.VMEM((2, 2 * FOX_HEADS, ck), F32),
            pltpu.VMEM((LANES, kvw), F32), pltpu.VMEM((LANES, kvw), F32),
            pltpu.SemaphoreType.DMA((3, 2)),
            pltpu.VMEM((nrow, 1), F32), pltpu.VMEM((nrow, 1), F32),
            pltpu.VMEM((nrow, kvw), F32), pltpu.VMEM((FOX_HEADS, LANES), F32)]


N_MLA_IN, N_FOX_IN = 9, 9
N_MLA_SCRATCH, N_FOX_SCRATCH = 10, 10


def _paged_kernel(pt_ref, *refs, n_seq, n_chunks, pages):
    ins = refs[:N_MLA_IN + N_FOX_IN]
    o_mla, o_fox = refs[N_MLA_IN + N_FOX_IN:N_MLA_IN + N_FOX_IN + 2]
    scratch = refs[N_MLA_IN + N_FOX_IN + 2:]
    mla = _mla_branch(pt_ref, *ins[:N_MLA_IN], o_mla, *scratch[:N_MLA_SCRATCH], n_chunks=n_chunks, pages=pages)
    fox = _fox_branch(pt_ref, *ins[N_MLA_IN:], o_fox, *scratch[N_MLA_SCRATCH:], n_chunks=n_chunks, pages=pages)
    b = pl.program_id(0)

    def start(seq, k, slot):
        for c in mla.copies(seq, k, slot) + fox.copies(seq, k, slot):
            c.start()

    def wait(slot):
        for c in mla.copies(0, 0, slot) + fox.copies(0, 0, slot):
            c.wait()

    def compute(slot):
        mla.compute(slot)
        fox.compute(slot)

    @pl.when(b == 0)
    def _():
        mla.init_once()
        fox.init_once()

    mla.init_seq()
    fox.init_seq()
    _chunk_loop(b, n_seq, n_chunks, start, wait, compute)
    mla.finish()
    fox.finish()


def _paged_call(pt, mla_in, fox_in):
    n_seq, n_pages = pt.shape
    pages = min(PAGES_PER_CHUNK, n_pages // 2)
    n_chunks = n_pages // pages
    ck = pages * PAGE
    nrow = mla_in[0].shape[1]
    t_new = mla_in[2].shape[1]
    seq3 = lambda a: pl.BlockSpec((1,) + a.shape[1:], lambda b, pt_: (b, 0, 0))
    anyspec = pl.BlockSpec(memory_space=pl.ANY)
    whole = _vmem_whole()
    in_specs = ([seq3(a) for a in mla_in[:4]] + [whole, whole, whole, anyspec, anyspec]
                + [seq3(a) for a in fox_in[:5]] + [whole, anyspec, anyspec, anyspec])
    o_spec = lambda w: pl.BlockSpec((1, t_new, w), lambda b, pt_: (b, 0, 0))
    wm, wf = MLA_HEADS * MLA_V, FOX_HEADS * FOX_HEAD_DIM
    grid_spec = pltpu.PrefetchScalarGridSpec(
        num_scalar_prefetch=1, grid=(n_seq,),
        in_specs=in_specs,
        out_specs=[o_spec(wm), o_spec(wf)],
        scratch_shapes=_mla_scratch(ck, nrow, mla_in[4].shape[0]) + _fox_scratch(ck, nrow))
    return pl.pallas_call(
        functools.partial(_paged_kernel, n_seq=n_seq, n_chunks=n_chunks, pages=pages),
        out_shape=[jax.ShapeDtypeStruct((n_seq, t_new, wm), BF16), jax.ShapeDtypeStruct((n_seq, t_new, wf), BF16)],
        grid_spec=grid_spec,
        compiler_params=_cparams(("arbitrary",)),
        name="paged",
    )(pt, *mla_in, *fox_in)


def _merge_kernel(x_ref, om_ref, of_ref, sg_ref, m2_ref, m3_ref, m4_ref, gffn_ref,
                  wba_ref, wbb_ref, wo_ref, wrh_ref, wrl_ref, br_ref,
                  x1_ref, h2_ref, rt_ref, *, o_transposed):
    d = x_ref.shape[1]
    if o_transposed:
        tn = lambda o, w: lax.dot_general(o, w, (((0,), (0,)), ((), ())), preferred_element_type=F32)
        a = tn(om_ref[...], wba_ref[...])
        bb = tn(of_ref[...], wbb_ref[...])
    else:
        a = _dot(om_ref[...], wba_ref[...])
        bb = _dot(of_ref[...], wbb_ref[...])
    u = sg_ref[:, 0:d].astype(F32) * a + sg_ref[:, d:2 * d].astype(F32) * bb
    x1 = x_ref[...] + m2_ref[...] * _dot(u.astype(BF16), wo_ref[...])
    x1_ref[...] = x1
    xn = x1 * lax.rsqrt(jnp.mean(x1 * x1, axis=-1, keepdims=True) + RMS_EPS) * gffn_ref[...]
    h2 = xn * (1.0 + m4_ref[...]) + m3_ref[...]
    h2_ref[...] = h2
    h2_hi = h2.astype(BF16)
    h2_lo = (h2 - h2_hi.astype(F32)).astype(BF16)
    lg = (_dot(h2_hi, wrh_ref[...]) + _dot(h2_hi, wrl_ref[...]) + _dot(h2_lo, wrh_ref[...])
          + br_ref[...])
    lane = lax.broadcasted_iota(jnp.int32, lg.shape, 1)
    neg = -jnp.inf
    is_g = (lane >= N_EXPERTS) & (lane < N_EXPERTS + N_GROUPS)
    lgm = jnp.where(is_g, lg, neg)
    mg = jnp.max(lgm, axis=-1, keepdims=True)
    eg = jnp.exp(lgm - mg)
    pgrp = eg / jnp.sum(eg, axis=-1, keepdims=True)
    pg_top = jnp.max(pgrp, axis=-1, keepdims=True)
    g_idx = jnp.min(jnp.where(is_g & (pgrp == pg_top), lane - N_EXPERTS, N_GROUPS), axis=-1, keepdims=True)
    sel = (lane < N_EXPERTS) & (lane // EXPERTS_PER_GROUP == g_idx)
    em = jnp.where(sel, lg, neg)
    me = jnp.max(em, axis=-1, keepdims=True)
    ee = jnp.exp(em - me)
    pe = ee / jnp.sum(ee, axis=-1, keepdims=True)
    big = 4 * LANES
    p1 = jnp.max(jnp.where(sel, pe, -1.0), axis=-1, keepdims=True)
    i1 = jnp.min(jnp.where(sel & (pe == p1), lane, big), axis=-1, keepdims=True)
    sel2 = sel & (lane != i1)
    p2 = jnp.max(jnp.where(sel2, pe, -1.0), axis=-1, keepdims=True)
    i2 = jnp.min(jnp.where(sel2 & (pe == p2), lane, big), axis=-1, keepdims=True)
    den = p1 + p2
    w1 = pg_top * p1 / den
    w2 = pg_top * p2 / den
    rt = jnp.where(lane == 0, i1.astype(F32),
                   jnp.where(lane == 1, i2.astype(F32),
                             jnp.where(lane == 2, w1, jnp.where(lane == 3, w2, 0.0))))
    rt_ref[...] = rt


def _merge_call(x2d, o_mla, o_fox, sg, m2, m3, m4, mod_specs, consts, *, o_transposed):
    n, d = x2d.shape
    tm = min(TM_PROJ, n)
    row = lambda c: pl.BlockSpec((tm, c), lambda i: (i, 0))
    col = lambda r: pl.BlockSpec((r, tm), lambda i: (0, i))
    o_spec = (lambda o: col(o.shape[0])) if o_transposed else (lambda o: row(o.shape[1]))
    in_specs = [row(d), o_spec(o_mla), o_spec(o_fox), row(sg.shape[1]),
                mod_specs[0], mod_specs[1], mod_specs[2]] + [_vmem_whole()] * 7
    return pl.pallas_call(
        functools.partial(_merge_kernel, o_transposed=o_transposed),
        out_shape=[jax.ShapeDtypeStruct((n, d), F32), jax.ShapeDtypeStruct((n, d), F32),
                   jax.ShapeDtypeStruct((n, LANES), F32)],
        grid=(n // tm,),
        in_specs=in_specs,
        out_specs=[row(d), row(d), row(LANES)],
        compiler_params=_cparams(("arbitrary",)),
        name="merge",
    )(x2d, o_mla, o_fox, sg, m2, m3, m4, *consts["merge"])


def _gmm_kernel(te_ref, nv_ref, idx_ref, idx1_ref, idx2_ref, h2_ref, wg_ref, wu_ref, wd_ref, o_ref,
                xbuf, wgb, wub, wdb, sem):
    i = pl.program_id(0)
    nv = nv_ref[0]
    valid = i < nv
    slot = lax.rem(i, GMM_SLOTS)
    rows = xbuf.shape[1]

    def row_copies(src_ref, s):
        return [pltpu.make_async_copy(h2_ref.at[pl.ds(src_ref[0, 0, r], 1)], xbuf.at[s, pl.ds(r, 1)], sem.at[s])
                for r in range(rows)]

    @pl.when(i == 0)
    def _():
        for c in row_copies(idx_ref, 0):
            c.start()

    @pl.when(jnp.logical_and(i == 0, nv > 1))
    def _():
        for c in row_copies(idx1_ref, 1):
            c.start()

    @pl.when(valid)
    def _():
        for c in row_copies(idx_ref, slot):
            c.wait()

    @pl.when(i + 2 < nv)
    def _():
        for c in row_copies(idx2_ref, lax.rem(i + 2, GMM_SLOTS)):
            c.start()

    prev = te_ref[jnp.maximum(i - 1, 0)]
    changed = jnp.logical_or(i == 0, te_ref[i] != prev)

    @pl.when(jnp.logical_and(valid, changed))
    def _():
        wgb[...] = wg_ref[0].astype(BF16)
        wub[...] = wu_ref[0].astype(BF16)
        wdb[...] = wd_ref[0].astype(BF16)

    @pl.when(valid)
    def _():
        x = xbuf[slot].astype(BF16)
        g = _dot(x, wgb[...])
        u = _dot(x, wub[...])
        hid = (g / (1.0 + jnp.exp(-g))) * u
        o_ref[...] = _dot(hid.astype(BF16), wdb[...]).astype(BF16)

    @pl.when(jnp.logical_not(valid))
    def _():
        o_ref[...] = jnp.zeros_like(o_ref)


def _gmm_call(tile_e, n_valid, src_tok, h2, w_gate, w_up, w_down):
    n_tiles = src_tok.shape[0]
    d = h2.shape[1]
    de = w_gate.shape[2]
    idx_spec = lambda f: pl.BlockSpec((1, 1, TG_MOE), f, memory_space=pltpu.SMEM)
    grid_spec = pltpu.PrefetchScalarGridSpec(
        num_scalar_prefetch=2, grid=(n_tiles,),
        in_specs=[idx_spec(lambda i, te, nv: (i, 0, 0)),
                  idx_spec(lambda i, te, nv: (jnp.minimum(i + 1, n_tiles - 1), 0, 0)),
                  idx_spec(lambda i, te, nv: (jnp.minimum(i + 2, n_tiles - 1), 0, 0)),
                  pl.BlockSpec(memory_space=pl.ANY),
                  pl.BlockSpec((1, d, de), lambda i, te, nv: (te[i], 0, 0)),
                  pl.BlockSpec((1, d, de), lambda i, te, nv: (te[i], 0, 0)),
                  pl.BlockSpec((1, de, d), lambda i, te, nv: (te[i], 0, 0))],
        out_specs=pl.BlockSpec((TG_MOE, d), lambda i, te, nv: (i, 0)),
        scratch_shapes=[pltpu.VMEM((GMM_SLOTS, TG_MOE, d), F32),
                        pltpu.VMEM((d, de), BF16), pltpu.VMEM((d, de), BF16), pltpu.VMEM((de, d), BF16),
                        pltpu.SemaphoreType.DMA((GMM_SLOTS,))])
    return pl.pallas_call(
        _gmm_kernel,
        out_shape=jax.ShapeDtypeStruct((n_tiles * TG_MOE, d), BF16),
        grid_spec=grid_spec,
        compiler_params=_cparams(("arbitrary",)),
        name="gmm",
    )(tile_e, n_valid, src_tok, src_tok, src_tok, h2, w_gate, w_up, w_down)


def _final_kernel(x1_ref, ya_ref, yb_ref, rt_ref, m5_ref, o_ref):
    rt = rt_ref[...]
    ffn = rt[:, 2:3] * ya_ref[...].astype(F32) + rt[:, 3:4] * yb_ref[...].astype(F32)
    o_ref[...] = x1_ref[...] + m5_ref[...] * ffn


def _final_call(x1, ya, yb, rt, m5, m5_spec_fn):
    n, d = x1.shape
    tm = min(TM_FINAL, n)
    row = lambda c: pl.BlockSpec((tm, c), lambda i: (i, 0))
    return pl.pallas_call(
        _final_kernel,
        out_shape=jax.ShapeDtypeStruct((n, d), F32),
        grid=(n // tm,),
        in_specs=[row(d), row(d), row(d), row(LANES), m5_spec_fn(tm)],
        out_specs=row(d),
        compiler_params=_cparams(("arbitrary",)),
        name="final",
    )(x1, ya, yb, rt, m5)


def _np_indicator(n_rows, groups):
    e = np.zeros((n_rows, LANES), np.float32)
    for col, (lo, hi) in enumerate(groups):
        e[lo:hi, col] = 1.0
    return e


def _static_consts():
    c = {}
    nq = MLA_HEADS * LANES
    eq = _np_indicator(nq, [(h * LANES, h * LANES + MLA_NOPE) for h in range(MLA_HEADS)]
                       + [(h * LANES + MLA_NOPE, h * LANES + MLA_NOPE + MLA_ROPE) for h in range(MLA_HEADS)])
    cnt = np.ones((1, LANES), np.float32)
    cnt[0, :MLA_HEADS] = 1.0 / MLA_NOPE
    cnt[0, MLA_HEADS:2 * MLA_HEADS] = 1.0 / MLA_ROPE
    ef = _np_indicator(nq, [(h * LANES, h * LANES + FOX_HEAD_DIM) for h in range(FOX_HEADS)])
    ek = _np_indicator(FOX_KV_HEADS * LANES, [(g * LANES, g * LANES + FOX_HEAD_DIM) for g in range(FOX_KV_HEADS)])
    pkr = np.zeros((LANES, nq), np.float32)
    for h in range(MLA_HEADS):
        for j in range(MLA_ROPE):
            pkr[j, h * LANES + MLA_NOPE + j] = 1.0
    pq = np.zeros((3 * LANES, FOX_HEADS * LANES), np.float32)
    pk = np.zeros((3 * LANES, FOX_KV_HEADS * LANES), np.float32)
    onesq = np.zeros((1, FOX_HEADS * LANES), np.float32)
    onesk = np.zeros((1, FOX_KV_HEADS * LANES), np.float32)
    rep = FOX_HEADS // FOX_KV_HEADS
    for h in range(FOX_HEADS):
        g, e = h // rep, h % rep
        for piece in range(3):
            pq[piece * LANES + LF0 + h, h * LANES + AUG0 + piece] = 1.0
            pk[piece * LANES + LF0 + h, g * LANES + AUG0 + 3 + 3 * e + piece] = -1.0
            onesq[0, h * LANES + AUG0 + 3 + 3 * e + piece] = 1.0
    for g in range(FOX_KV_HEADS):
        onesk[0, g * LANES + AUG0:g * LANES + AUG0 + 3] = 1.0
    psel = np.zeros((FOX_KV_HEADS * FOX_HEAD_DIM, FOX_HEADS * FOX_HEAD_DIM), np.float32)
    for h in range(FOX_HEADS):
        g = h // rep
        for dd in range(FOX_HEAD_DIM):
            psel[g * FOX_HEAD_DIM + dd, h * FOX_HEAD_DIM + dd] = 1.0
    b = lambda a: jnp.asarray(a, BF16)
    kk = np.arange(PAGE)
    c.update(triu=b(kk[:, None] > kk[None, :]), ones=b(np.ones((PAGE, PAGE), np.float32)))
    twice = lambda e: b(np.concatenate([e.T, e.T], axis=0))
    c.update(eq=b(eq), eqt=twice(eq), cntq=jnp.asarray(cnt), ef=b(ef), eft=twice(ef), ek=b(ek), ekt=twice(ek),
             pkr=b(pkr), pq=b(pq), pk=b(pk), onesq=jnp.asarray(onesq), onesk=jnp.asarray(onesk), psel=b(psel))
    return c


def _pad_heads(w, nh, dh):
    k = w.shape[0]
    return jnp.pad(w.reshape(k, nh, dh), ((0, 0), (0, 0), (0, LANES - dh))).reshape(k, nh * LANES)


def _head_row(g, nh, scale=1.0):
    dh = g.shape[0]
    return jnp.tile(jnp.pad(g * scale, (0, LANES - dh)), nh)[None, :]


def _rope_tables(pos):
    half = MLA_ROPE // 2
    inv = ROPE_THETA ** (-jnp.arange(half, dtype=F32) / half)
    ang = pos.astype(F32)[:, None] * inv[None, :]
    cos, sin = jnp.cos(ang), jnp.sin(ang)
    t = pos.shape[0]
    z = lambda n: jnp.zeros((t, n), F32)
    cosq = jnp.concatenate([jnp.ones((t, MLA_NOPE), F32), cos, cos, z(LANES - MLA_NOPE - MLA_ROPE)], axis=1)
    sinq = jnp.concatenate([z(MLA_NOPE), -sin, sin, z(LANES - MLA_NOPE - MLA_ROPE)], axis=1)
    cosk = jnp.concatenate([cos, cos, z(LANES - MLA_ROPE)], axis=1)
    sink = jnp.concatenate([-sin, sin, z(LANES - MLA_ROPE)], axis=1)
    return jnp.stack([cosq, sinq, cosk, sink], axis=0)


def kernel(x_prompt, x_sample, cache_mla_latent, cache_mla_krope, cache_fox_k, cache_fox_v, cache_fox_logf,
           page_table, c_prompt, c_sample, w_ada, b_ada, g_attn, w_in, g_qa, w_qb, g_qn, g_qr, g_kva, g_kr,
           w_uk, w_uv, g_kn, g_fq, g_fk, b_f, w_br_a, w_br_b, w_o, g_ffn, w_rg, b_rg, w_re, b_re,
           w_e_gate, w_e_up, w_e_down):
    depth = w_ada.shape[0]
    assert depth == 1, "single-layer trunk"
    bsz, seq, d = x_prompt.shape
    dbsz, dseq, _ = x_sample.shape
    n_pool = cache_mla_latent.shape[1]
    past = page_table.shape[1] * PAGE
    n_p, n_s = bsz * seq, dbsz * dseq
    n_tok = n_p + n_s
    sc = _static_consts()
    half = MLA_ROPE // 2
    scale_mla = float((MLA_NOPE + MLA_ROPE) ** -0.5)
    scale_fox = float(FOX_HEAD_DIM ** -0.5)

    n_c = bsz + dbsz
    n_c_pad = -(-n_c // 8) * 8
    c_all = jnp.pad(jnp.concatenate([c_prompt, c_sample], axis=0), ((0, n_c_pad - n_c), (0, 0)))
    mod = _mod_call(c_all, w_ada[0], b_ada[0][None, :])
    mod3 = mod.reshape(n_c_pad, 1, N_MOD * d)
    mod_s = jnp.repeat(mod[bsz:bsz + dbsz], dseq, axis=0)

    def mod_spec_p(k, tm):
        return pl.BlockSpec((None, 1, d), lambda i: ((i * tm) // seq, 0, k))

    def mod_spec_s(k, tm):
        return pl.BlockSpec((tm, d), lambda i: (i, k))

    qa_w, kvl_w, kr_w, fq_w, fk_w, fv_w, fl_w, ga_w, gb_w = jnp.split(
        w_in[0], np.cumsum([MLA_Q_RANK, MLA_KV_RANK, MLA_ROPE, FOX_HEADS * FOX_HEAD_DIM,
                            FOX_KV_HEADS * FOX_HEAD_DIM, FOX_KV_HEADS * FOX_HEAD_DIM, FOX_HEADS, d]).tolist(), axis=1)
    kr_sw = jnp.concatenate([kr_w[:, half:], kr_w[:, :half]], axis=1)
    misc_w = jnp.concatenate([kr_w, kr_sw, fl_w, jnp.zeros((d, LANES - 2 * MLA_ROPE - FOX_HEADS), F32)], axis=1)
    w_small = jnp.concatenate([qa_w, kvl_w, _pad_heads(fq_w, FOX_HEADS, FOX_HEAD_DIM),
                               _pad_heads(fk_w, FOX_KV_HEADS, FOX_HEAD_DIM), fv_w, misc_w], axis=1).astype(BF16)
    w_gates = jnp.concatenate([ga_w, gb_w], axis=1).astype(BF16)
    dq = MLA_NOPE + MLA_ROPE
    wqb3 = w_qb[0].reshape(MLA_Q_RANK, MLA_HEADS, dq)
    wqb_p = jnp.pad(wqb3, ((0, 0), (0, 0), (0, LANES - dq))).reshape(MLA_Q_RANK, MLA_HEADS * LANES)
    wqb_sw3 = jnp.concatenate([jnp.zeros((MLA_Q_RANK, MLA_HEADS, MLA_NOPE), F32),
                               wqb3[:, :, MLA_NOPE + half:], wqb3[:, :, MLA_NOPE:MLA_NOPE + half],
                               jnp.zeros((MLA_Q_RANK, MLA_HEADS, LANES - dq), F32)], axis=2)
    wqb_cat = jnp.concatenate([wqb_p, wqb_sw3.reshape(MLA_Q_RANK, MLA_HEADS * LANES)], axis=1).astype(BF16)
    wuk2 = w_uk[0].reshape(MLA_KV_RANK, MLA_HEADS * MLA_NOPE)
    wuv2 = w_uv[0].reshape(MLA_KV_RANK, MLA_HEADS * MLA_V)
    wkv = _pad_heads(wuk2, MLA_HEADS, MLA_NOPE).astype(BF16)

    def rows_aug(w, nh, dh):
        return jnp.pad(w.T.reshape(nh, dh, w.shape[0]), ((0, 0), (0, DV_AUG - dh), (0, 0))).reshape(nh * DV_AUG, -1)

    def ones_aug(nh, dh):
        o = np.zeros((nh, DV_AUG, 1), np.float32)
        o[:, dh:, :] = 1.0
        return jnp.asarray(o.reshape(nh * DV_AUG, 1))

    wuvt = rows_aug(wuv2, MLA_HEADS, MLA_V).astype(BF16)
    wfvt = rows_aug(fv_w, FOX_KV_HEADS, FOX_HEAD_DIM).astype(BF16)

    gqr = g_qr[0]
    gq_a = jnp.concatenate([g_qn[0], gqr, jnp.zeros((LANES - dq,), F32)])
    gq_b = jnp.concatenate([jnp.zeros((MLA_NOPE,), F32), gqr[half:], gqr[:half], jnp.zeros((LANES - dq,), F32)])
    gq = jnp.stack([jnp.tile(gq_a, MLA_HEADS), jnp.tile(gq_b, MLA_HEADS)], axis=0) * scale_mla
    gkr0 = g_kr[0]
    gkr = jnp.stack([jnp.pad(gkr0, (0, LANES - MLA_ROPE)),
                     jnp.pad(jnp.concatenate([gkr0[half:], gkr0[:half]]), (0, LANES - MLA_ROPE))], axis=0)
    bf_row = jnp.zeros((1, LANES), F32).at[0, LF0:LF0 + FOX_HEADS].set(b_f[0])
    proj_a = [g_attn, w_small, wqb_cat, wkv, g_qa, g_kva, gq, gkr,
              _head_row(g_fq[0], FOX_HEADS, scale_fox), _head_row(g_fk[0], FOX_KV_HEADS),
              _head_row(g_kn[0], MLA_HEADS), bf_row]
    proj_b = [sc["eq"], sc["eqt"], sc["cntq"], sc["ef"], sc["eft"], sc["ek"], sc["ekt"],
              sc["pkr"], sc["pq"], sc["pk"], sc["onesq"], sc["onesk"],
              wuvt, ones_aug(MLA_HEADS, MLA_V), wfvt, ones_aug(FOX_KV_HEADS, FOX_HEAD_DIM)]
    w_r = jnp.concatenate([w_re[0], w_rg[0], jnp.zeros((d, LANES - N_EXPERTS - N_GROUPS), F32)], axis=1)
    w_r_hi = w_r.astype(BF16)
    w_r_lo = (w_r - w_r_hi.astype(F32)).astype(BF16)
    b_r = jnp.concatenate([b_re[0], b_rg[0], jnp.zeros((LANES - N_EXPERTS - N_GROUPS,), F32)])[None, :]
    consts = {"proj_a": proj_a, "proj_b": proj_b,
              "merge": [g_ffn, w_br_a[0].astype(BF16), w_br_b[0].astype(BF16), w_o[0].astype(BF16),
                        w_r_hi, w_r_lo, b_r]}

    tm_p = min(TM_PROJ, n_p)
    tm_s = min(TM_PROJ, n_s)
    tabs_p = _rope_tables(jnp.arange(seq, dtype=jnp.int32))
    tabs_s = _rope_tables(past + (jnp.arange(tm_s, dtype=jnp.int32) % dseq))
    r_p = np.arange(tm_p)
    tri_p = jnp.asarray(r_p[None, :] <= r_p[:, None], BF16)
    r_s = np.arange(tm_s)
    tri_s = jnp.asarray((r_s[None, :] <= r_s[:, None]) & (r_s[None, :] // dseq == r_s[:, None] // dseq), BF16)
    xp2 = x_prompt.reshape(n_p, d)
    xs2 = x_sample.reshape(n_s, d)
    (hb_p, qmla_p, lat_p, kr_p, qfa_p, kfp_p, kfa_p, vf_p, lf_p, cum_p, kmla_p, vmt_p, vft_p) = _proj_call(
        xp2, mod3, mod3, (mod_spec_p(0, tm_p), mod_spec_p(1, tm_p)), tabs_p, tri_p, consts,
        seq_rows=seq, with_kv=True)
    (hb_s, qmla_s, lat_s, kr_s, qfa_s, kfp_s, kfa_s, vf_s, lf_s, cum_s, _, _, _) = _proj_call(
        xs2, mod_s, mod_s, (mod_spec_s(0, tm_s), mod_spec_s(1, tm_s)), tabs_s, tri_s, consts,
        seq_rows=tm_s, with_kv=False)

    sg_p = _gates_call(hb_p, w_gates)
    sg_s = _gates_call(hb_s, w_gates)

    o_mla_p = _attn_call(qmla_p.reshape(bsz, seq, -1), kmla_p.reshape(bsz, seq, -1), vmt_p,
                         nkv=MLA_HEADS, rep=1, dv=MLA_V, name="attn_mla")
    o_fox_p = _attn_call(qfa_p.reshape(bsz, seq, -1), kfa_p.reshape(bsz, seq, -1), vft_p,
                         nkv=FOX_KV_HEADS, rep=FOX_HEADS // FOX_KV_HEADS, dv=FOX_HEAD_DIM, name="attn_fox")

    nrow = dseq * MLA_HEADS
    q4 = qmla_s.reshape(dbsz, dseq, MLA_HEADS, LANES)
    eye_h = jnp.eye(MLA_HEADS, dtype=BF16)
    qn_bd = (q4[:, :, :, None, :MLA_NOPE] * eye_h[None, None, :, :, None]).reshape(dbsz, nrow, MLA_HEADS * MLA_NOPE)
    qr = q4[..., MLA_NOPE:dq].reshape(dbsz, nrow, MLA_ROPE)
    wukt = wuk2.T
    gcol = jnp.tile(g_kn[0], MLA_HEADS)[:, None]
    mla_in = [qn_bd, qr, lat_s.reshape(dbsz, dseq, -1), kr_s[:, :MLA_ROPE].reshape(dbsz, dseq, MLA_ROPE),
              wukt, gcol, wuv2.astype(BF16), cache_mla_latent[0], cache_mla_krope[0].transpose(0, 2, 1)]

    qf4 = qfa_s.reshape(dbsz, dseq, FOX_HEADS, LANES)[..., :FOX_HEAD_DIM]
    rep = FOX_HEADS // FOX_KV_HEADS
    onehot_g = jnp.asarray(np.arange(FOX_HEADS)[:, None] // rep == np.arange(FOX_KV_HEADS)[None, :], BF16)
    qbd = (qf4[:, :, :, None, :] * onehot_g[None, None, :, :, None]).reshape(dbsz, nrow, FOX_KV_HEADS * FOX_HEAD_DIM)
    kf_s = kfp_s.reshape(n_s, FOX_KV_HEADS, LANES)[:, :, :FOX_HEAD_DIM]
    cn = cum_s[:, LF0:LF0 + FOX_HEADS].reshape(dbsz, dseq, FOX_HEADS)
    cn_col = cn.reshape(dbsz, nrow, 1)
    cn_t = jnp.pad(cn.transpose(0, 2, 1), ((0, 0), (0, 0), (0, LANES - dseq)))
    kvw = FOX_KV_HEADS * FOX_HEAD_DIM
    clft = cache_fox_logf[0].transpose(0, 2, 1)
    cet = _suffix_call(clft, sc["triu"], sc["ones"])
    ckt = cache_fox_k[0].transpose(0, 2, 3, 1).reshape(n_pool, kvw, PAGE)
    cvt = cache_fox_v[0].transpose(0, 2, 3, 1).reshape(n_pool, kvw, PAGE)
    fox_in = [qbd, kf_s.reshape(dbsz, dseq, kvw), vf_s.reshape(dbsz, dseq, kvw), cn_col, cn_t, sc["psel"],
              ckt, cvt, cet]
    o_mla_s, o_fox_s = _paged_call(page_table, mla_in, fox_in)
    o_mla_s = o_mla_s.reshape(n_s, -1)
    o_fox_s = o_fox_s.reshape(n_s, -1)

    x1_p, h2_p, rt_p = _merge_call(xp2, o_mla_p, o_fox_p, sg_p, mod3, mod3, mod3,
                                   (mod_spec_p(2, tm_p), mod_spec_p(3, tm_p), mod_spec_p(4, tm_p)), consts,
                                   o_transposed=True)
    x1_s, h2_s, rt_s = _merge_call(xs2, o_mla_s, o_fox_s, sg_s, mod_s, mod_s, mod_s,
                                   (mod_spec_s(2, tm_s), mod_spec_s(3, tm_s), mod_spec_s(4, tm_s)), consts,
                                   o_transposed=False)

    h2 = jnp.concatenate([h2_p, h2_s], axis=0)
    rt = jnp.concatenate([rt_p, rt_s], axis=0)
    e_flat = rt[:, 0:TOP_K].astype(jnp.int32).reshape(-1)
    n_pairs = n_tok * TOP_K
    onehot = (e_flat[:, None] == jnp.arange(N_EXPERTS, dtype=jnp.int32)[None, :]).astype(jnp.int32)
    incl = jnp.cumsum(onehot, axis=0)
    counts = incl[-1]
    rank = jnp.sum((incl - onehot) * onehot, axis=1)
    tiles_e = (counts + TG_MOE - 1) // TG_MOE
    tile_end = jnp.cumsum(tiles_e)
    tile_start = tile_end - tiles_e
    dest = jnp.sum(onehot * tile_start[None, :], axis=1) * TG_MOE + rank
    n_tiles = n_pairs // TG_MOE + N_EXPERTS
    p_total = n_tiles * TG_MOE
    src_tok = jnp.zeros((p_total,), jnp.int32).at[dest].set(jnp.arange(n_pairs, dtype=jnp.int32) // TOP_K,
                                                            unique_indices=True)
    n_valid = tile_end[-1:].astype(jnp.int32)
    tile_ids = jnp.arange(n_tiles, dtype=jnp.int32)
    tile_e = jnp.sum((jnp.minimum(tile_ids, n_valid[0] - 1)[:, None] >= tile_end[None, :]).astype(jnp.int32), axis=1)
    tile_e = jnp.minimum(tile_e, N_EXPERTS - 1).astype(jnp.int32)
    ys = _gmm_call(tile_e, n_valid, src_tok.reshape(n_tiles, 1, TG_MOE), h2, w_e_gate[0], w_e_up[0], w_e_down[0])
    dest2 = dest.reshape(n_tok, TOP_K)
    pick = lambda lo, hi, slot: jnp.take(ys, dest2[lo:hi, slot], axis=0, mode="clip")

    y_p = _final_call(x1_p, pick(0, n_p, 0), pick(0, n_p, 1), rt_p, mod3,
                      lambda tm: pl.BlockSpec((None, 1, d), lambda i: ((i * tm) // seq, 0, 5)))
    y_s = _final_call(x1_s, pick(n_p, n_tok, 0), pick(n_p, n_tok, 1), rt_s, mod_s,
                      lambda tm: pl.BlockSpec((tm, d), lambda i: (i, 5)))

    def leaves(lat, kr, kfp, vf, lf, bb, tt):
        return (lat.reshape(1, bb, tt, MLA_KV_RANK),
                kr[:, :MLA_ROPE].reshape(1, bb, tt, MLA_ROPE),
                kfp.reshape(-1, FOX_KV_HEADS, LANES)[:, :, :FOX_HEAD_DIM].reshape(1, bb, tt, FOX_KV_HEADS, FOX_HEAD_DIM),
                vf.reshape(1, bb, tt, FOX_KV_HEADS, FOX_HEAD_DIM),
                lf[:, LF0:LF0 + FOX_HEADS].reshape(1, bb, tt, FOX_HEADS))

    return ((y_p.reshape(bsz, seq, d), y_s.reshape(dbsz, dseq, d))
            + leaves(lat_p, kr_p, kfp_p, vf_p, lf_p, bsz, seq)
            + leaves(lat_s, kr_s, kfp_s, vf_s, lf_s, dbsz, dseq))
```

```python
import functools

import numpy as np
import jax
import jax.numpy as jnp
from jax import lax
from jax.experimental import pallas as pl
from jax.experimental.pallas import tpu as pltpu

F32 = jnp.float32
BF16 = jnp.bfloat16

MLA_HEADS, MLA_NOPE, MLA_ROPE, MLA_V = 8, 64, 32, 64
MLA_Q_RANK, MLA_KV_RANK = 512, 256
FOX_HEADS, FOX_KV_HEADS, FOX_HEAD_DIM = 8, 4, 64
N_GROUPS, EXPERTS_PER_GROUP, TOP_K = 4, 8, 2
N_EXPERTS = N_GROUPS * EXPERTS_PER_GROUP
N_MOD = 6
ROPE_THETA = 10000.0
RMS_EPS = 1e-6
PAGE = 128
LANES = 128
AUG0 = 64
LF0 = 64
DV_AUG = 80

VMEM_LIMIT = 56 * 1024 * 1024
TM_PROJ = 256
TM_GATES = 512
TN_GATES = 1024
TN_MOD = 1024
TQ_ATTN = 1024
PAGES_PER_CHUNK = 32
MLA_SUB_KEYS = 1024
PREPASS_PAGES = 256
TG_MOE = 256
GMM_SLOTS = 3
TM_FINAL = 512


def _cparams(sem):
    return pltpu.CompilerParams(dimension_semantics=sem, vmem_limit_bytes=VMEM_LIMIT)


def _vmem_whole():
    return pl.BlockSpec(memory_space=pltpu.VMEM)


def _dot(a, b):
    return jnp.dot(a, b, preferred_element_type=F32)


def _dot_nt(a, b):
    return lax.dot_general(a, b, (((1,), (1,)), ((), ())), preferred_element_type=F32)


def _split3(x):
    hi = x.astype(BF16)
    r1 = x - hi.astype(F32)
    mid = r1.astype(BF16)
    lo = (r1 - mid.astype(F32)).astype(BF16)
    return hi, mid, lo


def _group_rsqrt(v, e_ref, inv_cnt):
    ss = _dot((v * v).astype(BF16), e_ref[...])
    return lax.rsqrt(ss * inv_cnt + RMS_EPS)


def _group_bcast(r, et2_ref):
    rh = r.astype(BF16)
    rl = (r - rh.astype(F32)).astype(BF16)
    return _dot(jnp.concatenate([rh, rl], axis=1), et2_ref[...])


def _mod_kernel(c_ref, w_ref, b_ref, o_ref):
    c = c_ref[...]
    a = (c / (1.0 + jnp.exp(-c))).astype(BF16)
    o_ref[...] = _dot(a, w_ref[...].astype(BF16)) + b_ref[...]


def _mod_call(c_all, w_ada, b_ada):
    rows, d = c_all.shape
    n = w_ada.shape[1]
    return pl.pallas_call(
        _mod_kernel,
        out_shape=jax.ShapeDtypeStruct((rows, n), F32),
        grid=(n // TN_MOD,),
        in_specs=[pl.BlockSpec((rows, d), lambda j: (0, 0)),
                  pl.BlockSpec((d, TN_MOD), lambda j: (0, j)),
                  pl.BlockSpec((1, TN_MOD), lambda j: (0, j))],
        out_specs=pl.BlockSpec((rows, TN_MOD), lambda j: (0, j)),
        compiler_params=_cparams(("arbitrary",)),
        name="mod",
    )(c_all, w_ada, b_ada)


C_QA = (0, 512)
C_KVL = (512, 768)
C_FQ = (768, 1792)
C_FK = (1792, 2304)
C_FV = (2304, 2560)
C_MISC = (2560, 2688)


def _proj_kernel(x_ref, m0_ref, m1_ref, gattn_ref, wsm_ref, wqb_ref, wkv_ref,
                 gqa_ref, gkva_ref, gq_ref, gkr_ref, gfq_ref, gfk_ref, gkn_ref, bf_ref,
                 tab_ref, eq_ref, eqt_ref, cntq_ref, ef_ref, eft_ref, ek_ref, ekt_ref,
                 pkr_ref, pq_ref, pk_ref, onesq_ref, onesk_ref, wuvt_ref, onesm_ref, wfvt_ref, onesf_ref,
                 tri_ref,
                 hb_ref, qmla_ref, lat_ref, kr_ref, qfa_ref, kfp_ref, kfa_ref, vf_ref,
                 lf_ref, cum_ref, kmla_ref, vmt_ref, vft_ref, carry_ref, *, tiles_per_seq, with_kv):
    i = pl.program_id(0)
    x = x_ref[...]
    xn = x * lax.rsqrt(jnp.mean(x * x, axis=-1, keepdims=True) + RMS_EPS) * gattn_ref[...]
    h = xn * (1.0 + m1_ref[...]) + m0_ref[...]
    hb = h.astype(BF16)
    hb_ref[...] = hb
    z = _dot(hb, wsm_ref[...])
    qa = z[:, C_QA[0]:C_QA[1]]
    kvl = z[:, C_KVL[0]:C_KVL[1]]
    fq = z[:, C_FQ[0]:C_FQ[1]]
    fk = z[:, C_FK[0]:C_FK[1]]
    fv = z[:, C_FV[0]:C_FV[1]]
    misc = z[:, C_MISC[0]:C_MISC[1]]
    cosq, sinq, cosk, sink = tab_ref[0], tab_ref[1], tab_ref[2], tab_ref[3]

    qan = qa * lax.rsqrt(jnp.mean(qa * qa, axis=-1, keepdims=True) + RMS_EPS) * gqa_ref[...]
    qq = _dot(qan.astype(BF16), wqb_ref[...])
    nq = MLA_HEADS * LANES
    q, qsw = qq[:, :nq], qq[:, nq:]
    sc = _group_bcast(_group_rsqrt(q, eq_ref, cntq_ref[...]), eqt_ref)
    gq = gq_ref[...]
    for hh in range(MLA_HEADS):
        sl = slice(hh * LANES, (hh + 1) * LANES)
        qo = sc[:, sl] * (q[:, sl] * (gq[0:1, sl] * cosq) + qsw[:, sl] * (gq[1:2, sl] * sinq))
        qmla_ref[:, sl] = qo.astype(BF16)

    lat = kvl * lax.rsqrt(jnp.mean(kvl * kvl, axis=-1, keepdims=True) + RMS_EPS) * gkva_ref[...]
    lat_ref[...] = lat

    lane = lax.broadcasted_iota(jnp.int32, misc.shape, 1)
    kr2 = jnp.where(lane < MLA_ROPE, misc * misc, 0.0)
    rk = lax.rsqrt(jnp.sum(kr2, axis=-1, keepdims=True) * (1.0 / MLA_ROPE) + RMS_EPS)
    msw = pltpu.roll(misc, LANES - MLA_ROPE, 1)
    gkr = gkr_ref[...]
    kr = rk * (misc * (gkr[0:1] * cosk) + msw * (gkr[1:2] * sink))
    kr_ref[...] = kr

    xl = misc + bf_ref[...]
    lf = jnp.minimum(xl, 0.0) - jnp.log1p(jnp.exp(-jnp.abs(xl)))
    lf = jnp.where((lane >= LF0) & (lane < LF0 + FOX_HEADS), lf, 0.0)
    lf_ref[...] = lf
    l_hi, l_mid, l_lo = _split3(lf)
    tri = tri_ref[...]
    cum = _dot(tri, l_hi) + _dot(tri, l_mid) + _dot(tri, l_lo)
    if tiles_per_seq > 1:
        @pl.when(i % tiles_per_seq == 0)
        def _():
            carry_ref[...] = jnp.zeros_like(carry_ref)
        cum = cum + carry_ref[...]
        carry_ref[...] = cum[cum.shape[0] - 1:, :]
    cum_ref[...] = cum
    c_hi, c_mid, c_lo = _split3(cum)
    cs = jnp.concatenate([c_hi, c_mid, c_lo], axis=1)
    augq = _dot(cs, pq_ref[...]) + onesq_ref[...]
    augk = _dot(cs, pk_ref[...]) + onesk_ref[...]

    scf = _group_bcast(_group_rsqrt(fq, ef_ref, 1.0 / FOX_HEAD_DIM), eft_ref)
    qfa_ref[...] = (fq * scf * gfq_ref[...] + augq).astype(BF16)
    sck = _group_bcast(_group_rsqrt(fk, ek_ref, 1.0 / FOX_HEAD_DIM), ekt_ref)
    kf = fk * sck * gfk_ref[...]
    kfp_ref[...] = kf
    kfa_ref[...] = (kf + augk).astype(BF16)
    vf_ref[...] = fv

    if with_kv:
        lat_b = lat.astype(BF16)
        kn = _dot(lat_b, wkv_ref[...])
        scn = _group_bcast(_group_rsqrt(kn, ef_ref, 1.0 / MLA_NOPE), eft_ref)
        kp = _dot(kr.astype(BF16), pkr_ref[...])
        kmla_ref[...] = (kn * scn * gkn_ref[...] + kp).astype(BF16)
        vmt_ref[...] = (_dot_nt(wuvt_ref[...], lat_b) + onesm_ref[...]).astype(BF16)
        vft_ref[...] = (_dot_nt(wfvt_ref[...], hb) + onesf_ref[...]).astype(BF16)
    else:
        kmla_ref[...] = jnp.zeros_like(kmla_ref)
        vmt_ref[...] = jnp.zeros_like(vmt_ref)
        vft_ref[...] = jnp.zeros_like(vft_ref)


def _proj_call(x2d, m0, m1, mod_specs, tabs, tri, consts, *, seq_rows, with_kv):
    n, d = x2d.shape
    tm = min(TM_PROJ, n)
    tiles_per_seq = max(seq_rows // tm, 1)
    n_tab_tiles = tabs.shape[1] // tm
    row = lambda c: pl.BlockSpec((tm, c), lambda i: (i, 0))
    in_specs = [row(d), mod_specs[0], mod_specs[1]] + [_vmem_whole()] * 12
    in_specs += [pl.BlockSpec((4, tm, LANES), lambda i: (0, i % n_tab_tiles, 0))]
    in_specs += [_vmem_whole()] * 16
    in_specs += [pl.BlockSpec((tm, tm), lambda i: (0, 0))]
    nq = MLA_HEADS * LANES
    out_cols = [(d, BF16), (nq, BF16), (MLA_KV_RANK, F32), (LANES, F32), (FOX_HEADS * LANES, BF16),
                (FOX_KV_HEADS * LANES, F32), (FOX_KV_HEADS * LANES, BF16),
                (FOX_KV_HEADS * FOX_HEAD_DIM, F32),
                (LANES, F32), (LANES, F32), (nq, BF16)]
    out_rows_t = [MLA_HEADS * DV_AUG, FOX_KV_HEADS * DV_AUG]
    col = lambda r: pl.BlockSpec((r, tm), lambda i: (0, i))
    return pl.pallas_call(
        functools.partial(_proj_kernel, tiles_per_seq=tiles_per_seq, with_kv=with_kv),
        out_shape=([jax.ShapeDtypeStruct((n, c), dt) for c, dt in out_cols]
                   + [jax.ShapeDtypeStruct((r, n), BF16) for r in out_rows_t]),
        grid=(n // tm,),
        in_specs=in_specs,
        out_specs=[row(c) for c, _ in out_cols] + [col(r) for r in out_rows_t],
        scratch_shapes=[pltpu.VMEM((1, LANES), F32)],
        compiler_params=_cparams(("arbitrary",)),
        name="proj",
    )(x2d, m0, m1, *consts["proj_a"], tabs, *consts["proj_b"], tri)


def _gates_kernel(h_ref, w_ref, o_ref):
    z = _dot(h_ref[...], w_ref[...])
    o_ref[...] = (1.0 / (1.0 + jnp.exp(-z))).astype(BF16)


def _gates_call(hb, wg):
    n, d = hb.shape
    nc = wg.shape[1]
    tm = min(TM_GATES, n)
    return pl.pallas_call(
        _gates_kernel,
        out_shape=jax.ShapeDtypeStruct((n, nc), BF16),
        grid=(nc // TN_GATES, n // tm),
        in_specs=[pl.BlockSpec((tm, d), lambda j, i: (i, 0)),
                  pl.BlockSpec((d, TN_GATES), lambda j, i: (0, j))],
        out_specs=pl.BlockSpec((tm, TN_GATES), lambda j, i: (i, j)),
        compiler_params=_cparams(("arbitrary", "arbitrary")),
        name="gates",
    )(hb, wg)


def _attn_kernel(ii_ref, jj_ref, q_ref, k_ref, vt_ref, o_ref, m_ref, acc_ref, *, nkv, rep, dv):
    i = ii_ref[pl.program_id(1)]
    j = jj_ref[pl.program_id(1)]
    tq = q_ref.shape[1]

    @pl.when(j == 0)
    def _():
        m_ref[...] = jnp.full_like(m_ref, -jnp.inf)
        acc_ref[...] = jnp.zeros_like(acc_ref)

    def update(masked):
        for g in range(nkv):
            k = k_ref[0, :, g * LANES:(g + 1) * LANES]
            qs = [q_ref[0, :, (g * rep + e) * LANES:(g * rep + e + 1) * LANES] for e in range(rep)]
            q = qs[0] if rep == 1 else jnp.concatenate(qs, axis=0)
            s = _dot_nt(k, q)
            if masked:
                rows = lax.broadcasted_iota(jnp.int32, s.shape, 0)
                cols = lax.broadcasted_iota(jnp.int32, s.shape, 1)
                s = jnp.where(rows <= cols % tq, s, -jnp.inf)
            m_prev = m_ref[g]
            m_new = jnp.maximum(m_prev, jnp.max(s, axis=0, keepdims=True))
            alpha = jnp.exp(m_prev - m_new)
            p = jnp.exp(s - m_new).astype(BF16)
            acc_ref[g] = alpha * acc_ref[g] + _dot(vt_ref[g * DV_AUG:(g + 1) * DV_AUG, :], p)
            m_ref[g] = m_new

    @pl.when(j < i)
    def _():
        update(False)

    @pl.when(j == i)
    def _():
        update(True)
        for g in range(nkv):
            for e in range(rep):
                a = acc_ref[g][:, e * tq:(e + 1) * tq]
                h = g * rep + e
                o_ref[h * dv:(h + 1) * dv, :] = (a[0:dv] / a[dv:dv + 1]).astype(BF16)


def _attn_call(q, k, vt, *, nkv, rep, dv, name):
    b, t, _ = q.shape
    tq = min(TQ_ATTN, t)
    nq = t // tq
    nh = nkv * rep
    pairs = [(i, j) for i in range(nq) for j in range(i + 1)]
    ii = jnp.asarray([p[0] for p in pairs], jnp.int32)
    jj = jnp.asarray([p[1] for p in pairs], jnp.int32)
    grid_spec = pltpu.PrefetchScalarGridSpec(
        num_scalar_prefetch=2, grid=(b, len(pairs)),
        in_specs=[pl.BlockSpec((1, tq, q.shape[2]), lambda bb, p, ii_, jj_: (bb, ii_[p], 0)),
                  pl.BlockSpec((1, tq, k.shape[2]), lambda bb, p, ii_, jj_: (bb, jj_[p], 0)),
                  pl.BlockSpec((nkv * DV_AUG, tq), lambda bb, p, ii_, jj_: (0, bb * nq + jj_[p]))],
        out_specs=pl.BlockSpec((nh * dv, tq), lambda bb, p, ii_, jj_: (0, bb * nq + ii_[p])),
        scratch_shapes=[pltpu.VMEM((nkv, 1, rep * tq), F32), pltpu.VMEM((nkv, DV_AUG, rep * tq), F32)])
    return pl.pallas_call(
        functools.partial(_attn_kernel, nkv=nkv, rep=rep, dv=dv),
        out_shape=jax.ShapeDtypeStruct((nh * dv, b * t), BF16),
        grid_spec=grid_spec,
        compiler_params=_cparams(("arbitrary", "arbitrary")),
        name=name,
    )(ii, jj, q, k, vt)


def _softmax_update(s, pv, m_ref, l_ref, acc_ref):
    m_prev = m_ref[...]
    m_new = jnp.maximum(m_prev, jnp.max(s, axis=-1, keepdims=True))
    alpha = jnp.exp(m_prev - m_new)
    p = jnp.exp(s - m_new)
    l_ref[...] = alpha * l_ref[...] + jnp.sum(p, axis=-1, keepdims=True)
    acc_ref[...] = alpha * acc_ref[...] + pv(p.astype(BF16))
    m_ref[...] = m_new


def _chunk_loop(b, n_seq, n_chunks, start, wait, compute):
    @pl.when(b == 0)
    def _():
        start(b, 0, 0)

    def pair(c2, carry):
        c = 2 * c2
        start(b, c + 1, 1)
        wait(0)
        compute(0)

        @pl.when(c + 2 < n_chunks)
        def _():
            start(b, c + 2, 0)

        @pl.when(jnp.logical_and(c + 2 >= n_chunks, b + 1 < n_seq))
        def _():
            start(b + 1, 0, 0)

        wait(1)
        compute(1)
        return carry

    lax.fori_loop(0, n_chunks // 2, pair, 0, unroll=True)


def _head_select(full, dv):
    rows = lax.broadcasted_iota(jnp.int32, full.shape, 0)
    cols = lax.broadcasted_iota(jnp.int32, full.shape, 1)
    nh = full.shape[1] // dv
    sel = jnp.where(cols // dv == rows % nh, full, 0.0)
    return jnp.sum(sel.reshape(full.shape[0] // nh, nh, full.shape[1]), axis=1)


class _Branch:
    def __init__(self, copies, init_once, init_seq, compute, finish):
        self.copies, self.init_once, self.init_seq, self.compute, self.finish = (
            copies, init_once, init_seq, compute, finish)


def _mla_branch(pt_ref, qn_ref, qr_ref, latn_ref, krn_ref, wukt_ref, gcol_ref, wuv_ref,
                clat_ref, ckr_ref, o_ref,
                a_ref, wg_ref, latbuf, krbuf, newlat, newkr, sem, m_ref, l_ref, acc_ref,
                *, n_chunks, pages):
    ck = pages * PAGE
    sub = min(MLA_SUB_KEYS, ck)
    nrow = qn_ref.shape[1]
    nup = wukt_ref.shape[0]
    t_new = latn_ref.shape[1]

    def copies(seq, k, slot):
        out = []
        for p in range(pages):
            pid = pt_ref[seq, k * pages + p]
            out.append(pltpu.make_async_copy(clat_ref.at[pid], latbuf.at[slot, pl.ds(p * PAGE, PAGE)],
                                             sem.at[0, slot]))
            out.append(pltpu.make_async_copy(ckr_ref.at[pid], krbuf.at[slot, :, pl.ds(p * PAGE, PAGE)],
                                             sem.at[1, slot]))
        return out

    def init_once():
        w = wukt_ref[...]
        a_ref[0:nup, :] = w.astype(BF16)
        wg_ref[...] = (w * gcol_ref[...]).astype(BF16)
        newlat[...] = jnp.zeros_like(newlat)
        newkr[...] = jnp.zeros_like(newkr)

    def init_seq():
        a_ref[nup:nup + nrow, :] = _dot(qn_ref[0], wg_ref[...]).astype(BF16)
        m_ref[...] = jnp.full_like(m_ref, -jnp.inf)
        l_ref[...] = jnp.zeros_like(l_ref)
        acc_ref[...] = jnp.zeros_like(acc_ref)

    def nope_scores(lat_b):
        r_all = _dot_nt(a_ref[...], lat_b)
        kup = r_all[0:nup]
        nk = kup.shape[1]
        ss = jnp.sum((kup * kup).reshape(MLA_HEADS, MLA_NOPE, nk), axis=1)
        rn = lax.rsqrt(ss * (1.0 / MLA_NOPE) + RMS_EPS)
        sn = r_all[nup:nup + nrow].reshape(nrow // MLA_HEADS, MLA_HEADS, nk) * rn[None]
        return sn.reshape(nrow, nk)

    def compute(slot):
        lat_b = latbuf[slot].astype(BF16)
        parts = [nope_scores(lat_b[j * sub:(j + 1) * sub]) for j in range(ck // sub)]
        s = jnp.concatenate(parts, axis=1) + _dot(qr_ref[0], krbuf[slot].astype(BF16))
        _softmax_update(s, lambda p: _dot(p, lat_b), m_ref, l_ref, acc_ref)

    def finish():
        newlat[0:t_new, :] = latn_ref[0]
        newkr[0:t_new, :] = krn_ref[0]
        lat_n = newlat[...].astype(BF16)
        s = nope_scores(lat_n) + _dot_nt(qr_ref[0], newkr[...].astype(BF16))
        rows = lax.broadcasted_iota(jnp.int32, s.shape, 0)
        cols = lax.broadcasted_iota(jnp.int32, s.shape, 1)
        s = jnp.where(cols <= rows // MLA_HEADS, s, -jnp.inf)
        _softmax_update(s, lambda p: _dot(p, lat_n), m_ref, l_ref, acc_ref)
        o_lat = (acc_ref[...] / l_ref[...]).astype(BF16)
        o_ref[0] = _head_select(_dot(o_lat, wuv_ref[...]), MLA_V).astype(BF16)

    return _Branch(copies, init_once, init_seq, compute, finish)


def _mla_scratch(ck, nrow, nup):
    return [pltpu.VMEM((nup + nrow, MLA_KV_RANK), BF16),
            pltpu.VMEM((nup, MLA_KV_RANK), BF16),
            pltpu.VMEM((2, ck, MLA_KV_RANK), F32),
            pltpu.VMEM((2, MLA_ROPE, ck), F32),
            pltpu.VMEM((LANES, MLA_KV_RANK), F32),
            pltpu.VMEM((LANES, MLA_ROPE), F32),
            pltpu.SemaphoreType.DMA((2, 2)),
            pltpu.VMEM((nrow, 1), F32), pltpu.VMEM((nrow, 1), F32),
            pltpu.VMEM((nrow, MLA_KV_RANK), F32)]


def _suffix_kernel(x_ref, triu_ref, ones_ref, o_ref):
    npg, nh, pg = x_ref.shape
    x = x_ref[...].reshape(npg * nh, pg)
    e = jnp.zeros(x.shape, F32)
    t = jnp.zeros(x.shape, F32)
    for piece in _split3(x):
        e = e + _dot(piece, triu_ref[...])
        t = t + _dot(piece, ones_ref[...])
    o_ref[:, 0:nh, :] = e.reshape(npg, nh, pg)
    o_ref[:, nh:2 * nh, :] = t.reshape(npg, nh, pg)


def _suffix_call(lft, triu, ones):
    n_pool, nh, pg = lft.shape
    npg = min(PREPASS_PAGES, n_pool)
    assert n_pool % npg == 0
    return pl.pallas_call(
        _suffix_kernel,
        out_shape=jax.ShapeDtypeStruct((n_pool, 2 * nh, pg), F32),
        grid=(n_pool // npg,),
        in_specs=[pl.BlockSpec((npg, nh, pg), lambda i: (i, 0, 0)), _vmem_whole(), _vmem_whole()],
        out_specs=pl.BlockSpec((npg, 2 * nh, pg), lambda i: (i, 0, 0)),
        compiler_params=_cparams(("arbitrary",)),
        name="suffix",
    )(lft, triu, ones)


def _fox_branch(pt_ref, qbd_ref, kn_ref, vn_ref, cncol_ref, cnt_ref, psel_ref,
                ck_ref, cv_ref, cet_ref, o_ref,
                kbuf, vbuf, etbuf, newk, newv, sem, m_ref, l_ref, acc_ref, sufc_ref,
                *, n_chunks, pages):
    ck = pages * PAGE
    nrow = qbd_ref.shape[1]
    t_new = kn_ref.shape[1]
    nt = nrow // FOX_HEADS

    def copies(seq, k, slot):
        out = []
        kk = n_chunks - 1 - k
        for p in range(pages):
            pid = pt_ref[seq, kk * pages + p]
            dst = pl.ds(p * PAGE, PAGE)
            out.append(pltpu.make_async_copy(ck_ref.at[pid], kbuf.at[slot, :, dst], sem.at[0, slot]))
            out.append(pltpu.make_async_copy(cv_ref.at[pid], vbuf.at[slot, :, dst], sem.at[1, slot]))
            out.append(pltpu.make_async_copy(cet_ref.at[pid], etbuf.at[slot, :, dst], sem.at[2, slot]))
        return out

    def init_once():
        newk[...] = jnp.zeros_like(newk)
        newv[...] = jnp.zeros_like(newv)

    def init_seq():
        m_ref[...] = jnp.full_like(m_ref, -jnp.inf)
        l_ref[...] = jnp.zeros_like(l_ref)
        acc_ref[...] = jnp.zeros_like(acc_ref)
        sufc_ref[...] = jnp.zeros_like(sufc_ref)

    def compute(slot):
        cncol = cncol_ref[0]
        s = _dot(qbd_ref[0], kbuf[slot].astype(BF16))
        carry = sufc_ref[...]
        pieces = [None] * pages
        for p in reversed(range(pages)):
            lanes_p = slice(p * PAGE, (p + 1) * PAGE)
            pieces[p] = etbuf[slot, 0:FOX_HEADS, lanes_p] + carry
            carry = carry + etbuf[slot, FOX_HEADS:2 * FOX_HEADS, lanes_p]
        sufc_ref[...] = carry
        bias = jnp.concatenate(pieces, axis=1)
        s = (s.reshape(nt, FOX_HEADS, ck) + bias[None]).reshape(nrow, ck) + cncol
        v_b = vbuf[slot].astype(BF16)
        _softmax_update(s, lambda p: _dot_nt(p, v_b), m_ref, l_ref, acc_ref)

    def finish():
        newk[0:t_new, :] = kn_ref[0]
        newv[0:t_new, :] = vn_ref[0]
        s = _dot_nt(qbd_ref[0], newk[...].astype(BF16))
        s = (s.reshape(nt, FOX_HEADS, LANES) - cnt_ref[0][None]).reshape(nrow, LANES) + cncol_ref[0]
        rows = lax.broadcasted_iota(jnp.int32, s.shape, 0)
        cols = lax.broadcasted_iota(jnp.int32, s.shape, 1)
        s = jnp.where(cols <= rows // FOX_HEADS, s, -jnp.inf)
        v_n = newv[...].astype(BF16)
        _softmax_update(s, lambda p: _dot(p, v_n), m_ref, l_ref, acc_ref)
        o = (acc_ref[...] / l_ref[...]).astype(BF16)
        o_ref[0] = _head_select(_dot(o, psel_ref[...]), FOX_HEAD_DIM).astype(BF16)

    return _Branch(copies, init_once, init_seq, compute, finish)


def _fox_scratch(ck, nrow):
    kvw = FOX_KV_HEADS * FOX_HEAD_DIM
    return [pltpu.VMEM((2, kvw, ck), F32), pltpu.VMEM((2, kvw, ck), F32),
            pltpu.VMEM((2, 2 * FOX_HEADS, ck), F32),
            pltpu.VMEM((LANES, kvw), F32), pltpu.VMEM((LANES, kvw), F32),
            pltpu.SemaphoreType.DMA((3, 2)),
            pltpu.VMEM((nrow, 1), F32), pltpu.VMEM((nrow, 1), F32),
            pltpu.VMEM((nrow, kvw), F32), pltpu.VMEM((FOX_HEADS, LANES), F32)]


N_MLA_IN, N_FOX_IN = 9, 9
N_MLA_SCRATCH, N_FOX_SCRATCH = 10, 10


def _paged_kernel(pt_ref, *refs, n_seq, n_chunks, pages):
    ins = refs[:N_MLA_IN + N_FOX_IN]
    o_mla, o_fox = refs[N_MLA_IN + N_FOX_IN:N_MLA_IN + N_FOX_IN + 2]
    scratch = refs[N_MLA_IN + N_FOX_IN + 2:]
    mla = _mla_branch(pt_ref, *ins[:N_MLA_IN], o_mla, *scratch[:N_MLA_SCRATCH], n_chunks=n_chunks, pages=pages)
    fox = _fox_branch(pt_ref, *ins[N_MLA_IN:], o_fox, *scratch[N_MLA_SCRATCH:], n_chunks=n_chunks, pages=pages)
    b = pl.program_id(0)

    def start(seq, k, slot):
        for c in mla.copies(seq, k, slot) + fox.copies(seq, k, slot):
            c.start()

    def wait(slot):
        for c in mla.copies(0, 0, slot) + fox.copies(0, 0, slot):
            c.wait()

    def compute(slot):
        mla.compute(slot)
        fox.compute(slot)

    @pl.when(b == 0)
    def _():
        mla.init_once()
        fox.init_once()

    mla.init_seq()
    fox.init_seq()
    _chunk_loop(b, n_seq, n_chunks, start, wait, compute)
    mla.finish()
    fox.finish()


def _paged_call(pt, mla_in, fox_in):
    n_seq, n_pages = pt.shape
    pages = min(PAGES_PER_CHUNK, n_pages // 2)
    n_chunks = n_pages // pages
    ck = pages * PAGE
    nrow = mla_in[0].shape[1]
    t_new = mla_in[2].shape[1]
    seq3 = lambda a: pl.BlockSpec((1,) + a.shape[1:], lambda b, pt_: (b, 0, 0))
    anyspec = pl.BlockSpec(memory_space=pl.ANY)
    whole = _vmem_whole()
    in_specs = ([seq3(a) for a in mla_in[:4]] + [whole, whole, whole, anyspec, anyspec]
                + [seq3(a) for a in fox_in[:5]] + [whole, anyspec, anyspec, anyspec])
    o_spec = lambda w: pl.BlockSpec((1, t_new, w), lambda b, pt_: (b, 0, 0))
    wm, wf = MLA_HEADS * MLA_V, FOX_HEADS * FOX_HEAD_DIM
    grid_spec = pltpu.PrefetchScalarGridSpec(
        num_scalar_prefetch=1, grid=(n_seq,),
        in_specs=in_specs,
        out_specs=[o_spec(wm), o_spec(wf)],
        scratch_shapes=_mla_scratch(ck, nrow, mla_in[4].shape[0]) + _fox_scratch(ck, nrow))
    return pl.pallas_call(
        functools.partial(_paged_kernel, n_seq=n_seq, n_chunks=n_chunks, pages=pages),
        out_shape=[jax.ShapeDtypeStruct((n_seq, t_new, wm), BF16), jax.ShapeDtypeStruct((n_seq, t_new, wf), BF16)],
        grid_spec=grid_spec,
        compiler_params=_cparams(("arbitrary",)),
        name="paged",
    )(pt, *mla_in, *fox_in)


def _merge_kernel(x_ref, om_ref, of_ref, sg_ref, m2_ref, m3_ref, m4_ref, gffn_ref,
                  wba_ref, wbb_ref, wo_ref, wrh_ref, wrl_ref, br_ref,
                  x1_ref, h2_ref, rt_ref, *, o_transposed):
    d = x_ref.shape[1]
    if o_transposed:
        tn = lambda o, w: lax.dot_general(o, w, (((0,), (0,)), ((), ())), preferred_element_type=F32)
        a = tn(om_ref[...], wba_ref[...])
        bb = tn(of_ref[...], wbb_ref[...])
    else:
        a = _dot(om_ref[...], wba_ref[...])
        bb = _dot(of_ref[...], wbb_ref[...])
    u = sg_ref[:, 0:d].astype(F32) * a + sg_ref[:, d:2 * d].astype(F32) * bb
    x1 = x_ref[...] + m2_ref[...] * _dot(u.astype(BF16), wo_ref[...])
    x1_ref[...] = x1
    xn = x1 * lax.rsqrt(jnp.mean(x1 * x1, axis=-1, keepdims=True) + RMS_EPS) * gffn_ref[...]
    h2 = xn * (1.0 + m4_ref[...]) + m3_ref[...]
    h2_ref[...] = h2
    h2_hi = h2.astype(BF16)
    h2_lo = (h2 - h2_hi.astype(F32)).astype(BF16)
    lg = (_dot(h2_hi, wrh_ref[...]) + _dot(h2_hi, wrl_ref[...]) + _dot(h2_lo, wrh_ref[...])
          + br_ref[...])
    lane = lax.broadcasted_iota(jnp.int32, lg.shape, 1)
    neg = -jnp.inf
    is_g = (lane >= N_EXPERTS) & (lane < N_EXPERTS + N_GROUPS)
    lgm = jnp.where(is_g, lg, neg)
    mg = jnp.max(lgm, axis=-1, keepdims=True)
    eg = jnp.exp(lgm - mg)
    pgrp = eg / jnp.sum(eg, axis=-1, keepdims=True)
    pg_top = jnp.max(pgrp, axis=-1, keepdims=True)
    g_idx = jnp.min(jnp.where(is_g & (pgrp == pg_top), lane - N_EXPERTS, N_GROUPS), axis=-1, keepdims=True)
    sel = (lane < N_EXPERTS) & (lane // EXPERTS_PER_GROUP == g_idx)
    em = jnp.where(sel, lg, neg)
    me = jnp.max(em, axis=-1, keepdims=True)
    ee = jnp.exp(em - me)
    pe = ee / jnp.sum(ee, axis=-1, keepdims=True)
    big = 4 * LANES
    p1 = jnp.max(jnp.where(sel, pe, -1.0), axis=-1, keepdims=True)
    i1 = jnp.min(jnp.where(sel & (pe == p1), lane, big), axis=-1, keepdims=True)
    sel2 = sel & (lane != i1)
    p2 = jnp.max(jnp.where(sel2, pe, -1.0), axis=-1, keepdims=True)
    i2 = jnp.min(jnp.where(sel2 & (pe == p2), lane, big), axis=-1, keepdims=True)
    den = p1 + p2
    w1 = pg_top * p1 / den
    w2 = pg_top * p2 / den
    rt = jnp.where(lane == 0, i1.astype(F32),
                   jnp.where(lane == 1, i2.astype(F32),
                             jnp.where(lane == 2, w1, jnp.where(lane == 3, w2, 0.0))))
    rt_ref[...] = rt


def _merge_call(x2d, o_mla, o_fox, sg, m2, m3, m4, mod_specs, consts, *, o_transposed):
    n, d = x2d.shape
    tm = min(TM_PROJ, n)
    row = lambda c: pl.BlockSpec((tm, c), lambda i: (i, 0))
    col = lambda r: pl.BlockSpec((r, tm), lambda i: (0, i))
    o_spec = (lambda o: col(o.shape[0])) if o_transposed else (lambda o: row(o.shape[1]))
    in_specs = [row(d), o_spec(o_mla), o_spec(o_fox), row(sg.shape[1]),
                mod_specs[0], mod_specs[1], mod_specs[2]] + [_vmem_whole()] * 7
    return pl.pallas_call(
        functools.partial(_merge_kernel, o_transposed=o_transposed),
        out_shape=[jax.ShapeDtypeStruct((n, d), F32), jax.ShapeDtypeStruct((n, d), F32),
                   jax.ShapeDtypeStruct((n, LANES), F32)],
        grid=(n // tm,),
        in_specs=in_specs,
        out_specs=[row(d), row(d), row(LANES)],
        compiler_params=_cparams(("arbitrary",)),
        name="merge",
    )(x2d, o_mla, o_fox, sg, m2, m3, m4, *consts["merge"])


def _gmm_kernel(te_ref, nv_ref, idx_ref, idx1_ref, idx2_ref, h2_ref, wg_ref, wu_ref, wd_ref, o_ref,
                xbuf, wgb, wub, wdb, sem):
    i = pl.program_id(0)
    nv = nv_ref[0]
    valid = i < nv
    slot = lax.rem(i, GMM_SLOTS)
    rows = xbuf.shape[1]

    def row_copies(src_ref, s):
        return [pltpu.make_async_copy(h2_ref.at[pl.ds(src_ref[0, 0, r], 1)], xbuf.at[s, pl.ds(r, 1)], sem.at[s])
                for r in range(rows)]

    def start_rows(src_ref, s):
        for r, c in enumerate(row_copies(src_ref, s)):
            c.start(priority=r % 2)

    @pl.when(i == 0)
    def _():
        start_rows(idx_ref, 0)

    @pl.when(jnp.logical_and(i == 0, nv > 1))
    def _():
        start_rows(idx1_ref, 1)

    @pl.when(valid)
    def _():
        for c in row_copies(idx_ref, slot):
            c.wait()

    @pl.when(i + 2 < nv)
    def _():
        start_rows(idx2_ref, lax.rem(i + 2, GMM_SLOTS))

    prev = te_ref[jnp.maximum(i - 1, 0)]
    changed = jnp.logical_or(i == 0, te_ref[i] != prev)

    @pl.when(jnp.logical_and(valid, changed))
    def _():
        wgb[...] = wg_ref[0].astype(BF16)
        wub[...] = wu_ref[0].astype(BF16)
        wdb[...] = wd_ref[0].astype(BF16)

    @pl.when(valid)
    def _():
        x = xbuf[slot].astype(BF16)
        g = _dot(x, wgb[...])
        u = _dot(x, wub[...])
        hid = (g / (1.0 + jnp.exp(-g))) * u
        o_ref[...] = _dot(hid.astype(BF16), wdb[...]).astype(BF16)

    @pl.when(jnp.logical_not(valid))
    def _():
        o_ref[...] = jnp.zeros_like(o_ref)


def _gmm_call(tile_e, n_valid, src_tok, h2, w_gate, w_up, w_down):
    n_tiles = src_tok.shape[0]
    d = h2.shape[1]
    de = w_gate.shape[2]
    idx_spec = lambda f: pl.BlockSpec((1, 1, TG_MOE), f, memory_space=pltpu.SMEM)
    grid_spec = pltpu.PrefetchScalarGridSpec(
        num_scalar_prefetch=2, grid=(n_tiles,),
        in_specs=[idx_spec(lambda i, te, nv: (i, 0, 0)),
                  idx_spec(lambda i, te, nv: (jnp.minimum(i + 1, n_tiles - 1), 0, 0)),
                  idx_spec(lambda i, te, nv: (jnp.minimum(i + 2, n_tiles - 1), 0, 0)),
                  pl.BlockSpec(memory_space=pl.ANY),
                  pl.BlockSpec((1, d, de), lambda i, te, nv: (te[i], 0, 0)),
                  pl.BlockSpec((1, d, de), lambda i, te, nv: (te[i], 0, 0)),
                  pl.BlockSpec((1, de, d), lambda i, te, nv: (te[i], 0, 0))],
        out_specs=pl.BlockSpec((TG_MOE, d), lambda i, te, nv: (i, 0)),
        scratch_shapes=[pltpu.VMEM((GMM_SLOTS, TG_MOE, d), F32),
                        pltpu.VMEM((d, de), BF16), pltpu.VMEM((d, de), BF16), pltpu.VMEM((de, d), BF16),
                        pltpu.SemaphoreType.DMA((GMM_SLOTS,))])
    return pl.pallas_call(
        _gmm_kernel,
        out_shape=jax.ShapeDtypeStruct((n_tiles * TG_MOE, d), BF16),
        grid_spec=grid_spec,
        compiler_params=_cparams(("arbitrary",)),
        name="gmm",
    )(tile_e, n_valid, src_tok, src_tok, src_tok, h2, w_gate, w_up, w_down)


def _final_kernel(x1_ref, ya_ref, yb_ref, rt_ref, m5_ref, o_ref):
    rt = rt_ref[...]
    ffn = rt[:, 2:3] * ya_ref[...].astype(F32) + rt[:, 3:4] * yb_ref[...].astype(F32)
    o_ref[...] = x1_ref[...] + m5_ref[...] * ffn


def _final_call(x1, ya, yb, rt, m5, m5_spec_fn):
    n, d = x1.shape
    tm = min(TM_FINAL, n)
    row = lambda c: pl.BlockSpec((tm, c), lambda i: (i, 0))
    return pl.pallas_call(
        _final_kernel,
        out_shape=jax.ShapeDtypeStruct((n, d), F32),
        grid=(n // tm,),
        in_specs=[row(d), row(d), row(d), row(LANES), m5_spec_fn(tm)],
        out_specs=row(d),
        compiler_params=_cparams(("arbitrary",)),
        name="final",
    )(x1, ya, yb, rt, m5)


def _np_indicator(n_rows, groups):
    e = np.zeros((n_rows, LANES), np.float32)
    for col, (lo, hi) in enumerate(groups):
        e[lo:hi, col] = 1.0
    return e


def _static_consts():
    c = {}
    nq = MLA_HEADS * LANES
    eq = _np_indicator(nq, [(h * LANES, h * LANES + MLA_NOPE) for h in range(MLA_HEADS)]
                       + [(h * LANES + MLA_NOPE, h * LANES + MLA_NOPE + MLA_ROPE) for h in range(MLA_HEADS)])
    cnt = np.ones((1, LANES), np.float32)
    cnt[0, :MLA_HEADS] = 1.0 / MLA_NOPE
    cnt[0, MLA_HEADS:2 * MLA_HEADS] = 1.0 / MLA_ROPE
    ef = _np_indicator(nq, [(h * LANES, h * LANES + FOX_HEAD_DIM) for h in range(FOX_HEADS)])
    ek = _np_indicator(FOX_KV_HEADS * LANES, [(g * LANES, g * LANES + FOX_HEAD_DIM) for g in range(FOX_KV_HEADS)])
    pkr = np.zeros((LANES, nq), np.float32)
    for h in range(MLA_HEADS):
        for j in range(MLA_ROPE):
            pkr[j, h * LANES + MLA_NOPE + j] = 1.0
    pq = np.zeros((3 * LANES, FOX_HEADS * LANES), np.float32)
    pk = np.zeros((3 * LANES, FOX_KV_HEADS * LANES), np.float32)
    onesq = np.zeros((1, FOX_HEADS * LANES), np.float32)
    onesk = np.zeros((1, FOX_KV_HEADS * LANES), np.float32)
    rep = FOX_HEADS // FOX_KV_HEADS
    for h in range(FOX_HEADS):
        g, e = h // rep, h % rep
        for piece in range(3):
            pq[piece * LANES + LF0 + h, h * LANES + AUG0 + piece] = 1.0
            pk[piece * LANES + LF0 + h, g * LANES + AUG0 + 3 + 3 * e + piece] = -1.0
            onesq[0, h * LANES + AUG0 + 3 + 3 * e + piece] = 1.0
    for g in range(FOX_KV_HEADS):
        onesk[0, g * LANES + AUG0:g * LANES + AUG0 + 3] = 1.0
    psel = np.zeros((FOX_KV_HEADS * FOX_HEAD_DIM, FOX_HEADS * FOX_HEAD_DIM), np.float32)
    for h in range(FOX_HEADS):
        g = h // rep
        for dd in range(FOX_HEAD_DIM):
            psel[g * FOX_HEAD_DIM + dd, h * FOX_HEAD_DIM + dd] = 1.0
    b = lambda a: jnp.asarray(a, BF16)
    kk = np.arange(PAGE)
    c.update(triu=b(kk[:, None] > kk[None, :]), ones=b(np.ones((PAGE, PAGE), np.float32)))
    twice = lambda e: b(np.concatenate([e.T, e.T], axis=0))
    c.update(eq=b(eq), eqt=twice(eq), cntq=jnp.asarray(cnt), ef=b(ef), eft=twice(ef), ek=b(ek), ekt=twice(ek),
             pkr=b(pkr), pq=b(pq), pk=b(pk), onesq=jnp.asarray(onesq), onesk=jnp.asarray(onesk), psel=b(psel))
    return c


def _pad_heads(w, nh, dh):
    k = w.shape[0]
    return jnp.pad(w.reshape(k, nh, dh), ((0, 0), (0, 0), (0, LANES - dh))).reshape(k, nh * LANES)


def _head_row(g, nh, scale=1.0):
    dh = g.shape[0]
    return jnp.tile(jnp.pad(g * scale, (0, LANES - dh)), nh)[None, :]


def _rope_tables(pos):
    half = MLA_ROPE // 2
    inv = ROPE_THETA ** (-jnp.arange(half, dtype=F32) / half)
    ang = pos.astype(F32)[:, None] * inv[None, :]
    cos, sin = jnp.cos(ang), jnp.sin(ang)
    t = pos.shape[0]
    z = lambda n: jnp.zeros((t, n), F32)
    cosq = jnp.concatenate([jnp.ones((t, MLA_NOPE), F32), cos, cos, z(LANES - MLA_NOPE - MLA_ROPE)], axis=1)
    sinq = jnp.concatenate([z(MLA_NOPE), -sin, sin, z(LANES - MLA_NOPE - MLA_ROPE)], axis=1)
    cosk = jnp.concatenate([cos, cos, z(LANES - MLA_ROPE)], axis=1)
    sink = jnp.concatenate([-sin, sin, z(LANES - MLA_ROPE)], axis=1)
    return jnp.stack([cosq, sinq, cosk, sink], axis=0)


def kernel(x_prompt, x_sample, cache_mla_latent, cache_mla_krope, cache_fox_k, cache_fox_v, cache_fox_logf,
           page_table, c_prompt, c_sample, w_ada, b_ada, g_attn, w_in, g_qa, w_qb, g_qn, g_qr, g_kva, g_kr,
           w_uk, w_uv, g_kn, g_fq, g_fk, b_f, w_br_a, w_br_b, w_o, g_ffn, w_rg, b_rg, w_re, b_re,
           w_e_gate, w_e_up, w_e_down):
    depth = w_ada.shape[0]
    assert depth == 1, "single-layer trunk"
    bsz, seq, d = x_prompt.shape
    dbsz, dseq, _ = x_sample.shape
    n_pool = cache_mla_latent.shape[1]
    past = page_table.shape[1] * PAGE
    n_p, n_s = bsz * seq, dbsz * dseq
    n_tok = n_p + n_s
    sc = _static_consts()
    half = MLA_ROPE // 2
    scale_mla = float((MLA_NOPE + MLA_ROPE) ** -0.5)
    scale_fox = float(FOX_HEAD_DIM ** -0.5)

    n_c = bsz + dbsz
    n_c_pad = -(-n_c // 8) * 8
    c_all = jnp.pad(jnp.concatenate([c_prompt, c_sample], axis=0), ((0, n_c_pad - n_c), (0, 0)))
    mod = _mod_call(c_all, w_ada[0], b_ada[0][None, :])
    mod3 = mod.reshape(n_c_pad, 1, N_MOD * d)
    mod_s = jnp.repeat(mod[bsz:bsz + dbsz], dseq, axis=0)

    def mod_spec_p(k, tm):
        return pl.BlockSpec((None, 1, d), lambda i: ((i * tm) // seq, 0, k))

    def mod_spec_s(k, tm):
        return pl.BlockSpec((tm, d), lambda i: (i, k))

    qa_w, kvl_w, kr_w, fq_w, fk_w, fv_w, fl_w, ga_w, gb_w = jnp.split(
        w_in[0], np.cumsum([MLA_Q_RANK, MLA_KV_RANK, MLA_ROPE, FOX_HEADS * FOX_HEAD_DIM,
                            FOX_KV_HEADS * FOX_HEAD_DIM, FOX_KV_HEADS * FOX_HEAD_DIM, FOX_HEADS, d]).tolist(), axis=1)
    kr_sw = jnp.concatenate([kr_w[:, half:], kr_w[:, :half]], axis=1)
    misc_w = jnp.concatenate([kr_w, kr_sw, fl_w, jnp.zeros((d, LANES - 2 * MLA_ROPE - FOX_HEADS), F32)], axis=1)
    w_small = jnp.concatenate([qa_w, kvl_w, _pad_heads(fq_w, FOX_HEADS, FOX_HEAD_DIM),
                               _pad_heads(fk_w, FOX_KV_HEADS, FOX_HEAD_DIM), fv_w, misc_w], axis=1).astype(BF16)
    w_gates = jnp.concatenate([ga_w, gb_w], axis=1).astype(BF16)
    dq = MLA_NOPE + MLA_ROPE
    wqb3 = w_qb[0].reshape(MLA_Q_RANK, MLA_HEADS, dq)
    wqb_p = jnp.pad(wqb3, ((0, 0), (0, 0), (0, LANES - dq))).reshape(MLA_Q_RANK, MLA_HEADS * LANES)
    wqb_sw3 = jnp.concatenate([jnp.zeros((MLA_Q_RANK, MLA_HEADS, MLA_NOPE), F32),
                               wqb3[:, :, MLA_NOPE + half:], wqb3[:, :, MLA_NOPE:MLA_NOPE + half],
                               jnp.zeros((MLA_Q_RANK, MLA_HEADS, LANES - dq), F32)], axis=2)
    wqb_cat = jnp.concatenate([wqb_p, wqb_sw3.reshape(MLA_Q_RANK, MLA_HEADS * LANES)], axis=1).astype(BF16)
    wuk2 = w_uk[0].reshape(MLA_KV_RANK, MLA_HEADS * MLA_NOPE)
    wuv2 = w_uv[0].reshape(MLA_KV_RANK, MLA_HEADS * MLA_V)
    wkv = _pad_heads(wuk2, MLA_HEADS, MLA_NOPE).astype(BF16)

    def rows_aug(w, nh, dh):
        return jnp.pad(w.T.reshape(nh, dh, w.shape[0]), ((0, 0), (0, DV_AUG - dh), (0, 0))).reshape(nh * DV_AUG, -1)

    def ones_aug(nh, dh):
        o = np.zeros((nh, DV_AUG, 1), np.float32)
        o[:, dh:, :] = 1.0
        return jnp.asarray(o.reshape(nh * DV_AUG, 1))

    wuvt = rows_aug(wuv2, MLA_HEADS, MLA_V).astype(BF16)
    wfvt = rows_aug(fv_w, FOX_KV_HEADS, FOX_HEAD_DIM).astype(BF16)

    gqr = g_qr[0]
    gq_a = jnp.concatenate([g_qn[0], gqr, jnp.zeros((LANES - dq,), F32)])
    gq_b = jnp.concatenate([jnp.zeros((MLA_NOPE,), F32), gqr[half:], gqr[:half], jnp.zeros((LANES - dq,), F32)])
    gq = jnp.stack([jnp.tile(gq_a, MLA_HEADS), jnp.tile(gq_b, MLA_HEADS)], axis=0) * scale_mla
    gkr0 = g_kr[0]
    gkr = jnp.stack([jnp.pad(gkr0, (0, LANES - MLA_ROPE)),
                     jnp.pad(jnp.concatenate([gkr0[half:], gkr0[:half]]), (0, LANES - MLA_ROPE))], axis=0)
    bf_row = jnp.zeros((1, LANES), F32).at[0, LF0:LF0 + FOX_HEADS].set(b_f[0])
    proj_a = [g_attn, w_small, wqb_cat, wkv, g_qa, g_kva, gq, gkr,
              _head_row(g_fq[0], FOX_HEADS, scale_fox), _head_row(g_fk[0], FOX_KV_HEADS),
              _head_row(g_kn[0], MLA_HEADS), bf_row]
    proj_b = [sc["eq"], sc["eqt"], sc["cntq"], sc["ef"], sc["eft"], sc["ek"], sc["ekt"],
              sc["pkr"], sc["pq"], sc["pk"], sc["onesq"], sc["onesk"],
              wuvt, ones_aug(MLA_HEADS, MLA_V), wfvt, ones_aug(FOX_KV_HEADS, FOX_HEAD_DIM)]
    w_r = jnp.concatenate([w_re[0], w_rg[0], jnp.zeros((d, LANES - N_EXPERTS - N_GROUPS), F32)], axis=1)
    w_r_hi = w_r.astype(BF16)
    w_r_lo = (w_r - w_r_hi.astype(F32)).astype(BF16)
    b_r = jnp.concatenate([b_re[0], b_rg[0], jnp.zeros((LANES - N_EXPERTS - N_GROUPS,), F32)])[None, :]
    consts = {"proj_a": proj_a, "proj_b": proj_b,
              "merge": [g_ffn, w_br_a[0].astype(BF16), w_br_b[0].astype(BF16), w_o[0].astype(BF16),
                        w_r_hi, w_r_lo, b_r]}

    tm_p = min(TM_PROJ, n_p)
    tm_s = min(TM_PROJ, n_s)
    tabs_p = _rope_tables(jnp.arange(seq, dtype=jnp.int32))
    tabs_s = _rope_tables(past + (jnp.arange(tm_s, dtype=jnp.int32) % dseq))
    r_p = np.arange(tm_p)
    tri_p = jnp.asarray(r_p[None, :] <= r_p[:, None], BF16)
    r_s = np.arange(tm_s)
    tri_s = jnp.asarray((r_s[None, :] <= r_s[:, None]) & (r_s[None, :] // dseq == r_s[:, None] // dseq), BF16)
    xp2 = x_prompt.reshape(n_p, d)
    xs2 = x_sample.reshape(n_s, d)
    (hb_p, qmla_p, lat_p, kr_p, qfa_p, kfp_p, kfa_p, vf_p, lf_p, cum_p, kmla_p, vmt_p, vft_p) = _proj_call(
        xp2, mod3, mod3, (mod_spec_p(0, tm_p), mod_spec_p(1, tm_p)), tabs_p, tri_p, consts,
        seq_rows=seq, with_kv=True)
    (hb_s, qmla_s, lat_s, kr_s, qfa_s, kfp_s, kfa_s, vf_s, lf_s, cum_s, _, _, _) = _proj_call(
        xs2, mod_s, mod_s, (mod_spec_s(0, tm_s), mod_spec_s(1, tm_s)), tabs_s, tri_s, consts,
        seq_rows=tm_s, with_kv=False)

    sg_p = _gates_call(hb_p, w_gates)
    sg_s = _gates_call(hb_s, w_gates)

    o_mla_p = _attn_call(qmla_p.reshape(bsz, seq, -1), kmla_p.reshape(bsz, seq, -1), vmt_p,
                         nkv=MLA_HEADS, rep=1, dv=MLA_V, name="attn_mla")
    o_fox_p = _attn_call(qfa_p.reshape(bsz, seq, -1), kfa_p.reshape(bsz, seq, -1), vft_p,
                         nkv=FOX_KV_HEADS, rep=FOX_HEADS // FOX_KV_HEADS, dv=FOX_HEAD_DIM, name="attn_fox")

    nrow = dseq * MLA_HEADS
    q4 = qmla_s.reshape(dbsz, dseq, MLA_HEADS, LANES)
    eye_h = jnp.eye(MLA_HEADS, dtype=BF16)
    qn_bd = (q4[:, :, :, None, :MLA_NOPE] * eye_h[None, None, :, :, None]).reshape(dbsz, nrow, MLA_HEADS * MLA_NOPE)
    qr = q4[..., MLA_NOPE:dq].reshape(dbsz, nrow, MLA_ROPE)
    wukt = wuk2.T
    gcol = jnp.tile(g_kn[0], MLA_HEADS)[:, None]
    mla_in = [qn_bd, qr, lat_s.reshape(dbsz, dseq, -1), kr_s[:, :MLA_ROPE].reshape(dbsz, dseq, MLA_ROPE),
              wukt, gcol, wuv2.astype(BF16), cache_mla_latent[0], cache_mla_krope[0].transpose(0, 2, 1)]

    qf4 = qfa_s.reshape(dbsz, dseq, FOX_HEADS, LANES)[..., :FOX_HEAD_DIM]
    rep = FOX_HEADS // FOX_KV_HEADS
    onehot_g = jnp.asarray(np.arange(FOX_HEADS)[:, None] // rep == np.arange(FOX_KV_HEADS)[None, :], BF16)
    qbd = (qf4[:, :, :, None, :] * onehot_g[None, None, :, :, None]).reshape(dbsz, nrow, FOX_KV_HEADS * FOX_HEAD_DIM)
    kf_s = kfp_s.reshape(n_s, FOX_KV_HEADS, LANES)[:, :, :FOX_HEAD_DIM]
    cn = cum_s[:, LF0:LF0 + FOX_HEADS].reshape(dbsz, dseq, FOX_HEADS)
    cn_col = cn.reshape(dbsz, nrow, 1)
    cn_t = jnp.pad(cn.transpose(0, 2, 1), ((0, 0), (0, 0), (0, LANES - dseq)))
    kvw = FOX_KV_HEADS * FOX_HEAD_DIM
    clft = cache_fox_logf[0].transpose(0, 2, 1)
    cet = _suffix_call(clft, sc["triu"], sc["ones"])
    ckt = cache_fox_k[0].transpose(0, 2, 3, 1).reshape(n_pool, kvw, PAGE)
    cvt = cache_fox_v[0].transpose(0, 2, 3, 1).reshape(n_pool, kvw, PAGE)
    fox_in = [qbd, kf_s.reshape(dbsz, dseq, kvw), vf_s.reshape(dbsz, dseq, kvw), cn_col, cn_t, sc["psel"],
              ckt, cvt, cet]
    o_mla_s, o_fox_s = _paged_call(page_table, mla_in, fox_in)
    o_mla_s = o_mla_s.reshape(n_s, -1)
    o_fox_s = o_fox_s.reshape(n_s, -1)

    x1_p, h2_p, rt_p = _merge_call(xp2, o_mla_p, o_fox_p, sg_p, mod3, mod3, mod3,
                                   (mod_spec_p(2, tm_p), mod_spec_p(3, tm_p), mod_spec_p(4, tm_p)), consts,
                                   o_transposed=True)
    x1_s, h2_s, rt_s = _merge_call(xs2, o_mla_s, o_fox_s, sg_s, mod_s, mod_s, mod_s,
                                   (mod_spec_s(2, tm_s), mod_spec_s(3, tm_s), mod_spec_s(4, tm_s)), consts,
                                   o_transposed=False)

    h2 = jnp.concatenate([h2_p, h2_s], axis=0)
    rt = jnp.concatenate([rt_p, rt_s], axis=0)
    e_flat = rt[:, 0:TOP_K].astype(jnp.int32).reshape(-1)
    n_pairs = n_tok * TOP_K
    onehot = (e_flat[:, None] == jnp.arange(N_EXPERTS, dtype=jnp.int32)[None, :]).astype(jnp.int32)
    incl = jnp.cumsum(onehot, axis=0)
    counts = incl[-1]
    rank = jnp.sum((incl - onehot) * onehot, axis=1)
    tiles_e = (counts + TG_MOE - 1) // TG_MOE
    tile_end = jnp.cumsum(tiles_e)
    tile_start = tile_end - tiles_e
    dest = jnp.sum(onehot * tile_start[None, :], axis=1) * TG_MOE + rank
    n_tiles = n_pairs // TG_MOE + N_EXPERTS
    p_total = n_tiles * TG_MOE
    src_tok = jnp.zeros((p_total,), jnp.int32).at[dest].set(jnp.arange(n_pairs, dtype=jnp.int32) // TOP_K,
                                                            unique_indices=True)
    n_valid = tile_end[-1:].astype(jnp.int32)
    tile_ids = jnp.arange(n_tiles, dtype=jnp.int32)
    tile_e = jnp.sum((jnp.minimum(tile_ids, n_valid[0] - 1)[:, None] >= tile_end[None, :]).astype(jnp.int32), axis=1)
    tile_e = jnp.minimum(tile_e, N_EXPERTS - 1).astype(jnp.int32)
    ys = _gmm_call(tile_e, n_valid, src_tok.reshape(n_tiles, 1, TG_MOE), h2, w_e_gate[0], w_e_up[0], w_e_down[0])
    dest2 = dest.reshape(n_tok, TOP_K)
    pick = lambda lo, hi, slot: jnp.take(ys, dest2[lo:hi, slot], axis=0, mode="clip")

    y_p = _final_call(x1_p, pick(0, n_p, 0), pick(0, n_p, 1), rt_p, mod3,
                      lambda tm: pl.BlockSpec((None, 1, d), lambda i: ((i * tm) // seq, 0, 5)))
    y_s = _final_call(x1_s, pick(n_p, n_tok, 0), pick(n_p, n_tok, 1), rt_s, mod_s,
                      lambda tm: pl.BlockSpec((tm, d), lambda i: (i, 5)))

    def leaves(lat, kr, kfp, vf, lf, bb, tt):
        return (lat.reshape(1, bb, tt, MLA_KV_RANK),
                kr[:, :MLA_ROPE].reshape(1, bb, tt, MLA_ROPE),
                kfp.reshape(-1, FOX_KV_HEADS, LANES)[:, :, :FOX_HEAD_DIM].reshape(1, bb, tt, FOX_KV_HEADS, FOX_HEAD_DIM),
                vf.reshape(1, bb, tt, FOX_KV_HEADS, FOX_HEAD_DIM),
                lf[:, LF0:LF0 + FOX_HEADS].reshape(1, bb, tt, FOX_HEADS))

    return ((y_p.reshape(bsz, seq, d), y_s.reshape(dbsz, dseq, d))
            + leaves(lat_p, kr_p, kfp_p, vf_p, lf_p, bsz, seq)
            + leaves(lat_s, kr_s, kfp_s, vf_s, lf_s, dbsz, dseq))
```
